```python
import math
import jax
import jax.numpy as jnp
from jax import lax
import numpy as np

D_MODEL = 1024
BATCH = 8
SEQ = 4096
DEPTH = 2

F32 = jnp.float32
EPS = 1e-6
D_MIX = 2 * D_MODEL
GROUP_WIDTH = D_MIX // 4
CONV_WIDTH = 4
CHUNK = 64
GDN_HEADS = 4
GDN_DK = 128
GDN_DV = GROUP_WIDTH // GDN_HEADS
HGRN_HEADS = 4
HGRN_DK = 128
HGRN_DV = GROUP_WIDTH // HGRN_HEADS
LRU_WIDTH = GROUP_WIDTH
LRU_BLOCKS = 4
LRU_BLOCK_W = LRU_WIDTH // LRU_BLOCKS
LRU_C = 8.0
GLA_HEADS = 4
GLA_DK = 64
GLA_DV = GROUP_WIDTH // GLA_HEADS
GLA_GATE_RANK = 16
GLA_GATE_TAU = 16.0
PEER_HEADS = 8
PEER_NKEYS = 128
PEER_EXPERTS = PEER_NKEYS * PEER_NKEYS
PEER_QDIM = 256
PEER_TOPK = 16
PEER_TOKEN_BLOCK = 128
IN_SIZES = (
    GDN_HEADS * GDN_DK, GDN_HEADS * GDN_DK, GDN_HEADS * GDN_DV, GDN_HEADS * GDN_DV, GDN_HEADS, GDN_HEADS,
    HGRN_HEADS * HGRN_DK, HGRN_HEADS * HGRN_DK, HGRN_HEADS * HGRN_DV, HGRN_HEADS * HGRN_DV,
    LRU_WIDTH, LRU_WIDTH,
    GLA_HEADS * GLA_DK, GLA_HEADS * GLA_DK, GLA_HEADS * GLA_DV, GLA_HEADS * GLA_DV, GLA_GATE_RANK,
)
D_IN = sum(IN_SIZES)

kernel_name = "hybrid_parallel_heads_peer_block"


def rmsnorm(x, w):
    xf = x.astype(F32)
    y = xf * lax.rsqrt(jnp.mean(xf * xf, axis=-1, keepdims=True) + EPS) * w.astype(F32)
    return y.astype(x.dtype)


def l2norm(t):
    return t * lax.rsqrt(jnp.sum(t * t, axis=-1, keepdims=True) + EPS)


def to_heads(t, n_heads):
    b, s, _ = t.shape
    return t.reshape(b, s, n_heads, -1).transpose(0, 2, 1, 3)


def causal_depthwise_conv(x, w):
    width, ch = w.shape
    return lax.conv_general_dilated(
        x, w[:, None, :].astype(x.dtype), window_strides=(1,), padding=[(width - 1, 0)],
        dimension_numbers=("NWC", "WIO", "NWC"), feature_group_count=ch)


def gated_head_rmsnorm(o, gate, w):
    b, h, s, d = o.shape
    o = o * lax.rsqrt(jnp.mean(o * o, axis=-1, keepdims=True) + EPS) * w.astype(F32)
    o = o.transpose(0, 2, 1, 3).reshape(b, s, h * d)
    return (o * jax.nn.silu(gate.astype(F32))).astype(gate.dtype)


def to_chunks(a):
    b, h, s = a.shape[:3]
    return jnp.moveaxis(a.reshape(b, h, s // CHUNK, CHUNK, *a.shape[3:]), 2, 0)


def from_chunks(o):
    n, b, h, c, d = o.shape
    return jnp.moveaxis(o, 0, 2).reshape(b, h, n * c, d)


def chunk_gated_delta_rule(q, k, v, g, beta):
    b, h, s, dk = q.shape
    dv = v.shape[-1]
    qc, kc, vc = to_chunks(q * dk ** -0.5), to_chunks(k), to_chunks(v)
    gc, bc = to_chunks(g), to_chunks(beta)
    gcum = jnp.cumsum(gc, axis=-1)
    causal = jnp.tril(jnp.ones((CHUNK, CHUNK), bool))
    strict = jnp.tril(jnp.ones((CHUNK, CHUNK), bool), -1)
    decay = jnp.exp(jnp.where(causal, gcum[..., :, None] - gcum[..., None, :], -jnp.inf))
    k_beta = kc * bc[..., None]
    a_low = jnp.where(strict, jnp.einsum("nbhid,nbhjd->nbhij", k_beta, kc) * decay, 0.0)
    lower = a_low + jnp.eye(CHUNK, dtype=F32)
    u = lax.linalg.triangular_solve(lower, vc * bc[..., None], left_side=True, lower=True)
    w = lax.linalg.triangular_solve(lower, k_beta * jnp.exp(gcum)[..., None], left_side=True, lower=True)
    attn_qk = jnp.einsum("nbhid,nbhjd->nbhij", qc, kc) * decay
    q_dec = qc * jnp.exp(gcum)[..., None]
    k_dec = kc * jnp.exp(gcum[..., -1:] - gcum)[..., None]
    g_last = jnp.exp(gcum[..., -1])

    def step(state, inp):
        u_i, w_i, a_i, qd_i, kd_i, gl_i = inp
        v_new = u_i - jnp.einsum("bhck,bhkv->bhcv", w_i, state)
        o_i = jnp.einsum("bhck,bhkv->bhcv", qd_i, state) + jnp.einsum("bhij,bhjv->bhiv", a_i, v_new)
        state = state * gl_i[..., None, None] + jnp.einsum("bhck,bhcv->bhkv", kd_i, v_new)
        return state, o_i

    s0 = jnp.zeros((b, h, dk, dv), F32)
    _, o = lax.scan(step, s0, (u, w, attn_qk, q_dec, k_dec, g_last))
    return from_chunks(o)


def chunk_gated_linear_attention(q, k, v, log_g, scale):
    b, h, s, dk = q.shape
    dv = v.shape[-1]
    qc, kc, vc = to_chunks(q * scale), to_chunks(k), to_chunks(v)
    bc = jnp.cumsum(to_chunks(log_g), axis=-2)
    causal = jnp.tril(jnp.ones((CHUNK, CHUNK), bool))[:, :, None]

    def step(state, inp):
        q_i, k_i, v_i, b_i = inp
        o_inter = jnp.einsum("bhck,bhkv->bhcv", q_i * jnp.exp(b_i), state)
        diff = jnp.where(causal, b_i[:, :, :, None, :] - b_i[:, :, None, :, :], -jnp.inf)
        attn = jnp.einsum("bhik,bhjk,bhijk->bhij", q_i, k_i, jnp.exp(diff))
        o_i = o_inter + jnp.einsum("bhij,bhjv->bhiv", attn, v_i)
        b_last = b_i[:, :, -1:, :]
        state = state * jnp.exp(b_last[:, :, 0, :, None]) + jnp.einsum(
            "bhck,bhcv->bhkv", k_i * jnp.exp(b_last - b_i), v_i)
        return state, o_i

    s0 = jnp.zeros((b, h, dk, dv), F32)
    _, o = lax.scan(step, s0, (qc, kc, vc, bc))
    return from_chunks(o)


def linear_recurrence_scan(a, x):
    def combine(c1, c2):
        a1, b1 = c1
        a2, b2 = c2
        return a1 * a2, a2 * b1 + b2
    _, h = lax.associative_scan(combine, (a, x), axis=1)
    return h


def gated_deltanet_group(q, k, v, z, a, bta, conv_w, a_log, dt_bias, norm_w):
    qkv = jax.nn.silu(causal_depthwise_conv(jnp.concatenate([q, k, v], axis=-1), conv_w))
    q, k, v = jnp.split(qkv, [GDN_HEADS * GDN_DK, 2 * GDN_HEADS * GDN_DK], axis=-1)
    qh = l2norm(to_heads(q, GDN_HEADS).astype(F32))
    kh = l2norm(to_heads(k, GDN_HEADS).astype(F32))
    vh = to_heads(v, GDN_HEADS).astype(F32)
    beta = jax.nn.sigmoid(bta.astype(F32)).transpose(0, 2, 1)
    g = -(jnp.exp(a_log.astype(F32)) * jax.nn.softplus(a.astype(F32) + dt_bias.astype(F32)))
    o = chunk_gated_delta_rule(qh, kh, vh, g.transpose(0, 2, 1), beta)
    return gated_head_rmsnorm(o, z, norm_w)


def hgrn2_group(q, f, i, g, lower_bound, norm_w):
    lb = lower_bound.astype(F32)
    fl = f.astype(F32)
    log_f = jnp.log(lb + (1.0 - lb) * jax.nn.sigmoid(fl))
    k = (1.0 - lb) * jax.nn.sigmoid(-fl)
    qh = to_heads(jax.nn.silu(q.astype(F32)), HGRN_HEADS)
    o = chunk_gated_linear_attention(qh, to_heads(k, HGRN_HEADS), to_heads(i.astype(F32), HGRN_HEADS),
                                     to_heads(log_f, HGRN_HEADS), HGRN_DK ** -0.5)
    return gated_head_rmsnorm(o, g, norm_w)


def rglru_group(xb, gate, conv_w, conv_b, w_a, b_a, w_x, b_x, lam):
    xb = causal_depthwise_conv(xb, conv_w) + conv_b.astype(xb.dtype)
    b, s, _ = xb.shape
    xblk = xb.reshape(b, s, LRU_BLOCKS, LRU_BLOCK_W)
    r = jax.nn.sigmoid(jnp.einsum("btgi,gij->btgj", xblk, w_a).reshape(b, s, LRU_WIDTH).astype(F32) + b_a.astype(F32))
    i = jax.nn.sigmoid(jnp.einsum("btgi,gij->btgj", xblk, w_x).reshape(b, s, LRU_WIDTH).astype(F32) + b_x.astype(F32))
    log_a = -LRU_C * r * jax.nn.softplus(-lam.astype(F32))
    mult = jnp.sqrt(-jnp.expm1(2.0 * log_a))
    h = linear_recurrence_scan(jnp.exp(log_a), mult * i * xb.astype(F32))
    return (h * jax.nn.gelu(gate.astype(F32))).astype(gate.dtype)


def gla_group(q, k, v, g, lowrank, w_gate, b_gate, norm_w):
    log_alpha = jax.nn.log_sigmoid((lowrank @ w_gate).astype(F32) + b_gate.astype(F32)) / GLA_GATE_TAU
    o = chunk_gated_linear_attention(to_heads(q.astype(F32), GLA_HEADS), to_heads(k.astype(F32), GLA_HEADS),
                                     to_heads(v.astype(F32), GLA_HEADS), to_heads(log_alpha, GLA_HEADS), GLA_DK ** -0.5)
    return gated_head_rmsnorm(o, g, norm_w)


def hybrid_mixer(h, w_in, w_out, gdn_conv_w, gdn_a_log, gdn_dt_bias, gdn_norm, hgrn_lb, hgrn_norm,
                 lru_conv_w, lru_conv_b, lru_w_a, lru_b_a, lru_w_x, lru_b_x, lru_lambda,
                 gla_w_gate, gla_b_gate, gla_norm):
    proj = h @ w_in
    splits = [int(c) for c in np.cumsum(IN_SIZES)[:-1]]
    (aq, ak, av, az, aa, ab, bq, bf, bi, bg, cx, cg, dq, dk, dv, dg, dlr) = jnp.split(proj, splits, axis=-1)
    y_a = gated_deltanet_group(aq, ak, av, az, aa, ab, gdn_conv_w, gdn_a_log, gdn_dt_bias, gdn_norm)
    y_b = hgrn2_group(bq, bf, bi, bg, hgrn_lb, hgrn_norm)
    y_c = rglru_group(cx, cg, lru_conv_w, lru_conv_b, lru_w_a, lru_b_a, lru_w_x, lru_b_x, lru_lambda)
    y_d = gla_group(dq, dk, dv, dg, dlr, gla_w_gate, gla_b_gate, gla_norm)
    y = jnp.concatenate([y_a, y_b, y_c, y_d], axis=-1)
    return y @ w_out


def peer_ffn(h, w_q, sub_keys, u, v):
    b, s, d = h.shape
    tokens = h.reshape(-1, PEER_TOKEN_BLOCK, d)

    def block(xb):
        n = xb.shape[0]
        q = (xb @ w_q).astype(F32).reshape(n, PEER_HEADS, 2, PEER_QDIM // 2)
        scores = jnp.einsum("nhpd,hpkd->nhpk", q, sub_keys.astype(F32))
        top_s, top_i = lax.top_k(scores, PEER_TOPK)
        cand_s = (top_s[:, :, 0, :, None] + top_s[:, :, 1, None, :]).reshape(n, PEER_HEADS, -1)
        cand_i = (top_i[:, :, 0, :, None] * PEER_NKEYS + top_i[:, :, 1, None, :]).reshape(n, PEER_HEADS, -1)
        best_s, best_pos = lax.top_k(cand_s, PEER_TOPK)
        experts = jnp.take_along_axis(cand_i, best_pos, axis=-1)
        gates = jax.nn.softmax(best_s, axis=-1)
        u_sel = u[experts]
        v_sel = v[experts]
        act = jax.nn.gelu(jnp.einsum("nd,nhkd->nhk", xb, u_sel).astype(F32))
        coeff = (gates * act).astype(xb.dtype)
        return jnp.einsum("nhk,nhkd->nd", coeff, v_sel)

    return lax.map(block, tokens).reshape(b, s, d)


def setup_inputs(seed: int = 0) -> dict:
    key = jax.random.key(seed)
    ks = jax.random.split(key, 26)
    L, D = DEPTH, D_MODEL

    def nrm(k, shape, scale):
        return scale * jax.random.normal(k, shape, F32)

    def gain(k, shape):
        return 1.0 + 0.02 * jax.random.normal(k, shape, F32)

    dt = jnp.exp(jax.random.uniform(ks[7], (L, GDN_HEADS), F32, math.log(1e-3), math.log(1e-1)))
    a0 = jax.random.uniform(ks[18], (L, LRU_WIDTH), F32, 0.9, 0.999)
    s_lam = a0 ** (1.0 / LRU_C)
    return {
        "x": jax.random.normal(ks[0], (BATCH, SEQ, D), F32),
        "norm_mix": gain(ks[1], (L, D)),
        "norm_ffn": gain(ks[2], (L, D)),
        "norm_final": gain(ks[3], (D,)),
        "w_in": nrm(ks[4], (L, D, D_IN), D ** -0.5),
        "w_out": nrm(ks[5], (L, D_MIX, D), D_MIX ** -0.5),
        "gdn_conv_w": nrm(ks[6], (L, CONV_WIDTH, GDN_HEADS * (2 * GDN_DK + GDN_DV)), CONV_WIDTH ** -0.5),
        "gdn_a_log": jnp.log(jax.random.uniform(ks[8], (L, GDN_HEADS), F32, 1.0, 16.0)),
        "gdn_dt_bias": dt + jnp.log(-jnp.expm1(-dt)),
        "gdn_norm": gain(ks[9], (L, GDN_DV)),
        "hgrn_lb_logits": nrm(ks[10], (L, HGRN_HEADS * HGRN_DK), 0.1),
        "hgrn_norm": gain(ks[11], (L, HGRN_DV)),
        "lru_conv_w": nrm(ks[12], (L, CONV_WIDTH, LRU_WIDTH), CONV_WIDTH ** -0.5),
        "lru_conv_b": nrm(ks[13], (L, LRU_WIDTH), 0.01),
        "lru_w_a": nrm(ks[14], (L, LRU_BLOCKS, LRU_BLOCK_W, LRU_BLOCK_W), LRU_BLOCK_W ** -0.5),
        "lru_b_a": nrm(ks[15], (L, LRU_WIDTH), 0.01),
        "lru_w_x": nrm(ks[16], (L, LRU_BLOCKS, LRU_BLOCK_W, LRU_BLOCK_W), LRU_BLOCK_W ** -0.5),
        "lru_b_x": nrm(ks[17], (L, LRU_WIDTH), 0.01),
        "lru_lambda": jnp.log(s_lam) - jnp.log1p(-s_lam),
        "gla_w_gate": nrm(ks[19], (L, GLA_GATE_RANK, GLA_HEADS * GLA_DK), GLA_GATE_RANK ** -0.5),
        "gla_b_gate": nrm(ks[20], (L, GLA_HEADS * GLA_DK), 0.01),
        "gla_norm": gain(ks[21], (L, GLA_DV)),
        "peer_w_q": nrm(ks[22], (L, D, PEER_HEADS * PEER_QDIM), D ** -0.5),
        "peer_sub_keys": nrm(ks[23], (L, PEER_HEADS, 2, PEER_NKEYS, PEER_QDIM // 2), (PEER_QDIM // 2) ** -0.5),
        "peer_u": nrm(ks[24], (L, PEER_EXPERTS, D), D ** -0.5),
        "peer_v": nrm(ks[25], (L, PEER_EXPERTS, D), PEER_HEADS ** -0.5),
    }


def reference(x, norm_mix, norm_ffn, norm_final, w_in, w_out, gdn_conv_w, gdn_a_log, gdn_dt_bias, gdn_norm,
              hgrn_lb_logits, hgrn_norm, lru_conv_w, lru_conv_b, lru_w_a, lru_b_a, lru_w_x, lru_b_x, lru_lambda,
              gla_w_gate, gla_b_gate, gla_norm, peer_w_q, peer_sub_keys, peer_u, peer_v):
    lb_all = jnp.cumsum(jax.nn.softmax(hgrn_lb_logits.astype(F32), axis=0), axis=0)
    lb_all = lb_all - lb_all[0]
    for l in range(DEPTH):
        h = rmsnorm(x, norm_mix[l])
        x = x + hybrid_mixer(h, w_in[l], w_out[l], gdn_conv_w[l], gdn_a_log[l], gdn_dt_bias[l], gdn_norm[l],
                             lb_all[l], hgrn_norm[l], lru_conv_w[l], lru_conv_b[l], lru_w_a[l], lru_b_a[l],
                             lru_w_x[l], lru_b_x[l], lru_lambda[l], gla_w_gate[l], gla_b_gate[l], gla_norm[l]).astype(x.dtype)
        h = rmsnorm(x, norm_ffn[l])
        x = x + peer_ffn(h, peer_w_q[l], peer_sub_keys[l], peer_u[l], peer_v[l]).astype(x.dtype)
    return rmsnorm(x, norm_final)
```

```python
import functools
import math

import numpy as np
import jax
import jax.numpy as jnp
from jax import lax
from jax.experimental import pallas as pl
from jax.experimental.pallas import tpu as pltpu

F32 = jnp.float32
BF16 = jnp.bfloat16
EPS = 1e-6
CHUNK = 64
GROUP_W = 512
N_HEADS = 4
HEAD_DV = 128
GLA_DK = 64
GLA_RANK = 16
GLA_TAU = 16.0
LRU_C = 8.0
LRU_BLOCKS = 4
PEER_HEADS = 8
PEER_NKEYS = 128
PEER_TOPK = 16
PEER_SEL = PEER_HEADS * PEER_TOPK
NEG_BIG = -1e30
VMEM_LIMIT = 56 * 1024 * 1024

P_AQ, P_AK, P_AV, P_AZ, P_BQ, P_BF, P_BI, P_BG, P_CX, P_CG, P_DV, P_DG, P_DQK, P_SM = range(14)
P_BLOCKS = 14
P_WIDTH = P_BLOCKS * GROUP_W


def _dot(a, b):
    return jnp.dot(a, b, preferred_element_type=F32)


def _dot_nt(a, b):
    return lax.dot_general(a, b, (((1,), (1,)), ((), ())), preferred_element_type=F32)


def _split3(a):
    a1 = a.astype(BF16)
    r1 = a - a1.astype(F32)
    a2 = r1.astype(BF16)
    r2 = r1 - a2.astype(F32)
    return a1, a2, r2.astype(BF16)


def _mm3(a, b, nt=False):
    d = _dot_nt if nt else _dot
    a1, a2, _ = _split3(a)
    b1, b2, _ = _split3(b)
    return d(a1, b1) + (d(a1, b2) + d(a2, b1))


def _mm_exact_l(w_bf, x):
    x1, x2, x3 = _split3(x)
    return _dot(w_bf, x1) + (_dot(w_bf, x2) + _dot(w_bf, x3))


def _mm_exact_r(x, w_bf):
    x1, x2, x3 = _split3(x)
    return _dot(x1, w_bf) + (_dot(x2, w_bf) + _dot(x3, w_bf))


def _sigmoid(x):
    return 1.0 / (1.0 + jnp.exp(-x))


def _silu(x):
    return x * _sigmoid(x)


def _softplus(x):
    return jnp.maximum(x, 0.0) + jnp.log(1.0 + jnp.exp(-jnp.abs(x)))


def _gelu_tanh(x):
    return 0.5 * x * (1.0 + jnp.tanh(math.sqrt(2.0 / math.pi) * (x + 0.044715 * (x * x * x))))


def _rms(x, gain):
    return x * lax.rsqrt(jnp.mean(x * x, axis=-1, keepdims=True) + EPS) * gain


def _params(*sem):
    return pltpu.CompilerParams(dimension_semantics=sem, vmem_limit_bytes=VMEM_LIMIT)


def _norm_matmul_kernel(x_ref, g_ref, w_ref, o_ref, h_ref):
    @pl.when(pl.program_id(1) == 0)
    def _():
        h_ref[...] = _rms(x_ref[...], g_ref[...]).astype(BF16)

    o_ref[...] = _dot(h_ref[...], w_ref[...]).astype(o_ref.dtype)


def norm_matmul(x, gain, w_bf, tm, tn, out_dtype=F32):
    m, d = x.shape
    n = w_bf.shape[1]
    return pl.pallas_call(
        _norm_matmul_kernel,
        grid=(m // tm, n // tn),
        in_specs=[pl.BlockSpec((tm, d), lambda i, j: (i, 0)),
                  pl.BlockSpec((1, d), lambda i, j: (0, 0)),
                  pl.BlockSpec((d, tn), lambda i, j: (0, j))],
        out_specs=pl.BlockSpec((tm, tn), lambda i, j: (i, j)),
        out_shape=jax.ShapeDtypeStruct((m, n), out_dtype),
        scratch_shapes=[pltpu.VMEM((tm, d), BF16)],
        compiler_params=_params("parallel", "arbitrary"),
        name="norm_matmul",
    )(x, gain.reshape(1, d), w_bf)


def _out_proj_kernel(ya_ref, yb_ref, yc_ref, yd_ref, w_ref, x_ref, o_ref):
    acc = x_ref[...]
    for g, y_ref in enumerate((ya_ref, yb_ref, yc_ref, yd_ref)):
        acc = acc + _dot(y_ref[...], w_ref[g])
    o_ref[...] = acc


def out_proj(ys, w_bf, x, tm):
    m, d = x.shape
    y_spec = pl.BlockSpec((tm, GROUP_W), lambda i: (i, 0))
    return pl.pallas_call(
        _out_proj_kernel,
        grid=(m // tm,),
        in_specs=[y_spec, y_spec, y_spec, y_spec,
                  pl.BlockSpec((4, GROUP_W, d), lambda i: (0, 0, 0)),
                  pl.BlockSpec((tm, d), lambda i: (i, 0))],
        out_specs=pl.BlockSpec((tm, d), lambda i: (i, 0)),
        out_shape=jax.ShapeDtypeStruct((m, d), F32),
        compiler_params=_params("parallel"),
        name="out_proj",
    )(*ys, w_bf, x)


def _final_norm_kernel(x_ref, g_ref, o_ref):
    o_ref[...] = _rms(x_ref[...], g_ref[...])


def final_norm(x, gain, tm):
    m, d = x.shape
    return pl.pallas_call(
        _final_norm_kernel,
        grid=(m // tm,),
        in_specs=[pl.BlockSpec((tm, d), lambda i: (i, 0)), pl.BlockSpec((1, d), lambda i: (0, 0))],
        out_specs=pl.BlockSpec((tm, d), lambda i: (i, 0)),
        out_shape=jax.ShapeDtypeStruct((m, d), F32),
        compiler_params=_params("parallel"),
        name="final_norm",
    )(x, gain.reshape(1, d))


def _chunk_consts():
    i = np.arange(CHUNK)[:, None]
    t = np.arange(CHUNK)[None, :]
    tril = (t <= i).astype(np.float32)
    after = (t > i).astype(np.float32)
    levels, low, masks = [], [], []
    s = CHUNK // 2
    while s >= 1:
        pos = i % (2 * s)
        ref_row = (i // (2 * s)) * (2 * s) + s
        lower = pos >= s
        w = np.where(lower, (t > ref_row) & (t <= i), (t > i) & (t <= ref_row)).astype(np.float32)
        levels.append(w)
        low.append(np.broadcast_to(lower, (CHUNK, 128)).astype(np.float32))
        same = (i // (2 * s)) == (t // (2 * s))
        masks.append((same & lower & ((t % (2 * s)) < s)).astype(np.float32))
        s //= 2
    wall = np.concatenate([tril, after] + levels, axis=0)
    return (jnp.asarray(wall, BF16), jnp.asarray(np.stack(low), F32), jnp.asarray(np.stack(masks), F32),
            jnp.asarray(tril, BF16), jnp.asarray(tril.T, BF16))


N_LEVELS = int(math.log2(CHUNK))


def _gla_chunk(qs, kh, vh, lg, st, wall, low_ref, mask_ref):
    e = _mm_exact_l(wall, lg)
    bcum = e[0:CHUNK]
    brev = e[CHUNK:2 * CHUNK]
    blast = bcum[CHUNK - 1:CHUNK]
    dk = qs.shape[1]
    attn = jnp.where(_eye(), _dot_nt(qs.astype(BF16), kh.astype(BF16)), 0.0)
    for lv in range(N_LEVELS):
        f = jnp.exp(e[(2 + lv) * CHUNK:(3 + lv) * CHUNK])
        z = (jnp.where(low_ref[lv][:, 0:dk] > 0.5, qs, kh) * f).astype(BF16)
        attn = attn + jnp.where(mask_ref[lv] > 0.5, _dot_nt(z, z), 0.0)
    o = _dot_nt((qs * jnp.exp(bcum)).astype(BF16), st.astype(BF16)) + _dot(attn.astype(BF16), vh.astype(BF16))
    kdec = (kh * jnp.exp(brev)).astype(BF16)
    st_new = st * jnp.exp(blast) + _dot(vh.T.astype(BF16), kdec)
    return o, st_new


def _eye():
    r = lax.broadcasted_iota(jnp.int32, (CHUNK, CHUNK), 0)
    c = lax.broadcasted_iota(jnp.int32, (CHUNK, CHUNK), 1)
    return r == c


def _gated_out(o, gate, nw):
    o = o * lax.rsqrt(jnp.mean(o * o, axis=-1, keepdims=True) + EPS) * nw
    return o * _silu(gate)


def _hgrn_kernel(q_ref, f_ref, i_ref, g_ref, lb_ref, nw_ref, wall_ref, low_ref, mask_ref, o_ref, st_ref):
    @pl.when(pl.program_id(1) == 0)
    def _():
        st_ref[...] = jnp.zeros_like(st_ref)

    lb = lb_ref[...]
    fl = f_ref[...]
    sg = _sigmoid(fl)
    lg_all = jnp.log(lb + (1.0 - lb) * sg)
    k_all = (1.0 - lb) * (1.0 - sg)
    q_all = _silu(q_ref[...]) * (HEAD_DV ** -0.5)
    wall = wall_ref[...]
    for h in range(N_HEADS):
        sl = slice(h * HEAD_DV, (h + 1) * HEAD_DV)
        o, st_new = _gla_chunk(q_all[:, sl], k_all[:, sl], i_ref[:, sl], lg_all[:, sl], st_ref[h],
                               wall, low_ref, mask_ref)
        st_ref[h] = st_new
        o_ref[:, sl] = _gated_out(o, g_ref[:, sl], nw_ref[...]).astype(o_ref.dtype)


def _gla_kernel(qk_ref, v_ref, g_ref, sm_ref, wg_ref, bg_ref, nw_ref, wall_ref, low_ref, mask_ref, o_ref, st_ref):
    @pl.when(pl.program_id(1) == 0)
    def _():
        st_ref[...] = jnp.zeros_like(st_ref)

    gate_in = _mm3(sm_ref[...], wg_ref[...]) + bg_ref[...]
    lg_all = -_softplus(-gate_in) * (1.0 / GLA_TAU)
    wall = wall_ref[...]
    nq = N_HEADS * GLA_DK
    for h in range(N_HEADS):
        qs = qk_ref[:, h * GLA_DK:(h + 1) * GLA_DK] * (GLA_DK ** -0.5)
        kh = qk_ref[:, nq + h * GLA_DK:nq + (h + 1) * GLA_DK]
        sl = slice(h * HEAD_DV, (h + 1) * HEAD_DV)
        o, st_new = _gla_chunk(qs, kh, v_ref[:, sl], lg_all[:, h * GLA_DK:(h + 1) * GLA_DK], st_ref[h],
                               wall, low_ref, mask_ref)
        st_ref[h] = st_new
        o_ref[:, sl] = _gated_out(o, g_ref[:, sl], nw_ref[...]).astype(o_ref.dtype)


def _pspec(col_block, width=GROUP_W):
    return pl.BlockSpec((None, CHUNK, width), lambda b, c: (b, c, col_block))


def _full(shape):
    nd = len(shape)
    return pl.BlockSpec(shape, lambda b, c: (0,) * nd)


def _mixer_call(kernel, proj, col_specs, consts, scratch, name):
    bsz, t, _ = proj.shape
    return pl.pallas_call(
        kernel,
        grid=(bsz, t // CHUNK),
        in_specs=col_specs + [_full(c.shape) for c in consts],
        out_specs=pl.BlockSpec((None, CHUNK, GROUP_W), lambda b, c: (b, c, 0)),
        out_shape=jax.ShapeDtypeStruct((bsz, t, GROUP_W), BF16),
        scratch_shapes=scratch,
        compiler_params=_params("parallel", "arbitrary"),
        name=name,
    )(*([proj] * len(col_specs)), *consts)


def hgrn_mixer(proj, lb, nw, cc):
    wall, low, masks, _, _ = cc
    return _mixer_call(
        _hgrn_kernel, proj, [_pspec(P_BQ), _pspec(P_BF), _pspec(P_BI), _pspec(P_BG)],
        [lb.reshape(1, GROUP_W), nw.reshape(1, HEAD_DV), wall, low, masks],
        [pltpu.VMEM((N_HEADS, HEAD_DV, HEAD_DV), F32)], "hgrn2_mixer")


def gla_mixer(proj, w_gate, b_gate, nw, cc):
    wall, low, masks, _, _ = cc
    wg = jnp.zeros((128, N_HEADS * GLA_DK), F32).at[0:GLA_RANK].set(w_gate)
    return _mixer_call(
        _gla_kernel, proj, [_pspec(P_DQK), _pspec(P_DV), _pspec(P_DG), _pspec(P_SM * 4 + 1, 128)],
        [wg, b_gate.reshape(1, N_HEADS * GLA_DK), nw.reshape(1, HEAD_DV), wall, low, masks],
        [pltpu.VMEM((N_HEADS, HEAD_DV, GLA_DK), F32)], "gla_mixer")


def _causal_conv(x_ref, tail_ref, slot, cw):
    cur = x_ref[...]
    ext = jnp.concatenate([tail_ref[slot], cur], axis=0)
    y = ext[5:5 + CHUNK] * cw[0:1]
    for w in range(1, 4):
        y = y + ext[5 + w:5 + w + CHUNK] * cw[w:w + 1]
    tail_ref[slot] = cur[CHUNK - 8:CHUNK]
    return y


def _l2n(t):
    return t * lax.rsqrt(jnp.sum(t * t, axis=-1, keepdims=True) + EPS)


def _gdn_kernel(q_ref, k_ref, v_ref, z_ref, sm_ref, cw_ref, alog_ref, dtb_ref, nw_ref, tril_ref, trilt_ref,
                o_ref, tail_ref, s_ref):
    @pl.when(pl.program_id(1) == 0)
    def _():
        tail_ref[...] = jnp.zeros_like(tail_ref)
        s_ref[...] = jnp.zeros_like(s_ref)

    q_all = _silu(_causal_conv(q_ref, tail_ref, 0, cw_ref[0]))
    k_all = _silu(_causal_conv(k_ref, tail_ref, 1, cw_ref[1]))
    v_all = _silu(_causal_conv(v_ref, tail_ref, 2, cw_ref[2]))
    sm = sm_ref[...]
    g_full = -(jnp.exp(alog_ref[...]) * _softplus(sm + dtb_ref[...]))
    beta_full = _sigmoid(sm)
    gcum = _mm_exact_l(tril_ref[...], g_full)
    gcum_t = _mm_exact_r(g_full.T, trilt_ref[...])
    r = lax.broadcasted_iota(jnp.int32, (CHUNK, CHUNK), 0)
    c = lax.broadcasted_iota(jnp.int32, (CHUNK, CHUNK), 1)
    causal = r >= c
    strict = r > c
    eye = (r == c).astype(F32)
    scale = HEAD_DV ** -0.5
    for h in range(N_HEADS):
        sl = slice(h * HEAD_DV, (h + 1) * HEAD_DV)
        qh = _l2n(q_all[:, sl]) * scale
        kh = _l2n(k_all[:, sl])
        vh = v_all[:, sl]
        gc = gcum[:, h:h + 1]
        gl = gcum[CHUNK - 1:CHUNK, h:h + 1]
        beta = beta_full[:, N_HEADS + h:N_HEADS + h + 1]
        decay = jnp.exp(jnp.where(causal, gc - gcum_t[h:h + 1, :], NEG_BIG))
        kb = kh * beta
        x = -jnp.where(strict, _mm3(kb, kh, nt=True) * decay, 0.0)
        tinv = eye + x
        p = x
        for _ in range(N_LEVELS - 1):
            p = _mm3(p, p)
            tinv = tinv + _mm3(tinv, p)
        egc = jnp.exp(gc)
        u = _mm3(tinv, vh * beta)
        w = _mm3(tinv, kb * egc)
        s = s_ref[h]
        v_new = u - _mm3(w, s)
        attn = jnp.where(causal, _mm3(qh, kh, nt=True) * decay, 0.0)
        o = _mm3(qh * egc, s) + _mm3(attn, v_new)
        kdec = kh * jnp.exp(gl - gc)
        s_ref[h] = s * jnp.exp(gl) + _mm3(kdec.T, v_new)
        o_ref[:, sl] = _gated_out(o, z_ref[:, sl], nw_ref[...]).astype(o_ref.dtype)


def gdn_mixer(proj, conv_w, a_log, dt_bias, nw, cc):
    _, _, _, tril, trilt = cc
    cw = jnp.zeros((3, 8, GROUP_W), F32).at[:, 0:4].set(conv_w.reshape(4, 3, GROUP_W).transpose(1, 0, 2))
    alog = jnp.zeros((1, 128), F32).at[0, 0:N_HEADS].set(a_log)
    dtb = jnp.zeros((1, 128), F32).at[0, 0:N_HEADS].set(dt_bias)
    return _mixer_call(
        _gdn_kernel, proj,
        [_pspec(P_AQ), _pspec(P_AK), _pspec(P_AV), _pspec(P_AZ), _pspec(P_SM * 4, 128)],
        [cw, alog, dtb, nw.reshape(1, HEAD_DV), tril, trilt],
        [pltpu.VMEM((3, 8, GROUP_W), F32), pltpu.VMEM((N_HEADS, HEAD_DV, HEAD_DV), F32)], "gdn_mixer")


def _shift_rows(x, d, fill):
    rows = lax.broadcasted_iota(jnp.int32, x.shape, 0)
    return jnp.where(rows >= d, pltpu.roll(x, d, 0), fill)


def _lru_kernel(x_ref, g_ref, cw_ref, cb_ref, wa_ref, ba_ref, wx_ref, bx_ref, lam_ref, o_ref, tail_ref, h_ref):
    @pl.when(pl.program_id(1) == 0)
    def _():
        tail_ref[...] = jnp.zeros_like(tail_ref)
        h_ref[...] = jnp.zeros_like(h_ref)

    xb = _causal_conv(x_ref, tail_ref, 0, cw_ref[...]) + cb_ref[...]
    xb_bf = xb.astype(BF16)
    bw = GROUP_W // LRU_BLOCKS
    ra = jnp.concatenate([_dot(xb_bf[:, g * bw:(g + 1) * bw], wa_ref[g]) for g in range(LRU_BLOCKS)], axis=1)
    rx = jnp.concatenate([_dot(xb_bf[:, g * bw:(g + 1) * bw], wx_ref[g]) for g in range(LRU_BLOCKS)], axis=1)
    rg = _sigmoid(ra + ba_ref[...])
    ig = _sigmoid(rx + bx_ref[...])
    log_a = -LRU_C * rg * _softplus(-lam_ref[...])
    a = jnp.exp(log_a)
    b = jnp.sqrt(1.0 - jnp.exp(2.0 * log_a)) * ig * xb
    d = 1
    while d < CHUNK:
        b = a * _shift_rows(b, d, 0.0) + b
        a = a * _shift_rows(a, d, 1.0)
        d *= 2
    h = b + a * h_ref[...]
    h_ref[...] = h[CHUNK - 1:CHUNK]
    o_ref[...] = (h * _gelu_tanh(g_ref[...])).astype(o_ref.dtype)


def lru_mixer(proj, conv_w, conv_b, w_a, b_a, w_x, b_x, lam):
    cw = jnp.zeros((8, GROUP_W), F32).at[0:4].set(conv_w)
    row = lambda v: v.reshape(1, GROUP_W)
    return _mixer_call(
        _lru_kernel, proj, [_pspec(P_CX), _pspec(P_CG)],
        [cw, row(conv_b), w_a.astype(BF16), row(b_a), w_x.astype(BF16), row(b_x), row(lam)],
        [pltpu.VMEM((1, 8, GROUP_W), F32), pltpu.VMEM((1, GROUP_W), F32)], "rglru_mixer")


ROUTE_TB = 256


def _topk_rows(s, k):
    n = s.shape[0]
    rows = lax.broadcasted_iota(jnp.int32, s.shape, 0)
    vals, ids = [], []
    for _ in range(k):
        m = jnp.max(s, axis=0, keepdims=True)
        idx = jnp.min(jnp.where(s == m, rows, n), axis=0, keepdims=True)
        vals.append(m)
        ids.append(idx)
        s = jnp.where(rows == idx, NEG_BIG, s)
    return jnp.concatenate(vals, axis=0), jnp.concatenate(ids, axis=0)


def _route_kernel(x_ref, g_ref, wq_ref, sk_ref, ex_ref, gt_ref):
    h2 = _rms(x_ref[...], g_ref[...]).astype(BF16)
    q = _dot(h2, wq_ref[...])
    for h in range(PEER_HEADS):
        tops = []
        for p in range(2):
            col = (h * 2 + p) * PEER_NKEYS
            sc = _dot_nt(sk_ref[h * 2 + p], q[:, col:col + PEER_NKEYS].astype(BF16))
            tops.append(_topk_rows(sc, PEER_TOPK))
        (s1, i1), (s2, i2) = tops
        cand_s = jnp.concatenate([s1[a:a + 1] + s2 for a in range(PEER_TOPK)], axis=0)
        cand_i = jnp.concatenate([i1[a:a + 1] * PEER_NKEYS + i2 for a in range(PEER_TOPK)], axis=0)
        best_s, best_pos = _topk_rows(cand_s, PEER_TOPK)
        pos = lax.broadcasted_iota(jnp.int32, cand_s.shape, 0)
        experts = jnp.concatenate(
            [jnp.sum(jnp.where(pos == best_pos[r:r + 1], cand_i, 0), axis=0, keepdims=True)
             for r in range(PEER_TOPK)], axis=0)
        ew = jnp.exp(best_s - best_s[0:1])
        gates = ew / jnp.sum(ew, axis=0, keepdims=True)
        ex_ref[h * PEER_TOPK:(h + 1) * PEER_TOPK, :] = experts
        gt_ref[h * PEER_TOPK:(h + 1) * PEER_TOPK, :] = gates


def peer_route(x, gain, wq_bf, sk_bf):
    m, d = x.shape
    nq = wq_bf.shape[1]
    return pl.pallas_call(
        _route_kernel,
        grid=(m // ROUTE_TB,),
        in_specs=[pl.BlockSpec((ROUTE_TB, d), lambda i: (i, 0)),
                  pl.BlockSpec((1, d), lambda i: (0, 0)),
                  pl.BlockSpec((d, nq), lambda i: (0, 0)),
                  pl.BlockSpec(sk_bf.shape, lambda i: (0, 0, 0))],
        out_specs=[pl.BlockSpec((PEER_SEL, ROUTE_TB), lambda i: (0, i)),
                   pl.BlockSpec((PEER_SEL, ROUTE_TB), lambda i: (0, i))],
        out_shape=[jax.ShapeDtypeStruct((PEER_SEL, m), jnp.int32),
                   jax.ShapeDtypeStruct((PEER_SEL, m), F32)],
        compiler_params=_params("parallel"),
        name="peer_route",
    )(x, gain.reshape(1, d), wq_bf, sk_bf)


EXP_TB = 128
EXP_TG = 8
EXP_ROWS = EXP_TG * PEER_SEL


def _expert_kernel(ids_ref, x_ref, g_ref, gt_ref, uv_ref, o_ref, buf_ref, h_ref, sem):
    d = x_ref.shape[1]
    n_sub = EXP_TB // EXP_TG
    h_ref[...] = _rms(x_ref[...], g_ref[...])

    def issue(sub, slot):
        def body(r, carry):
            e = ids_ref[sub * EXP_ROWS + r]
            pltpu.make_async_copy(uv_ref.at[pl.ds(e, 1)], buf_ref.at[slot, pl.ds(r, 1)], sem.at[slot]).start()
            return carry
        lax.fori_loop(0, EXP_ROWS, body, 0, unroll=8)

    def wait(slot):
        pltpu.make_async_copy(uv_ref.at[pl.ds(0, EXP_ROWS)], buf_ref.at[slot], sem.at[slot]).wait()

    lane = lax.broadcasted_iota(jnp.int32, (PEER_SEL, EXP_TB), 1)
    issue(0, 0)

    def sub_block(sub, carry):
        slot = sub % 2

        @pl.when(sub + 1 < n_sub)
        def _():
            issue(sub + 1, 1 - slot)

        wait(slot)
        act = jnp.zeros((PEER_SEL, EXP_TB), F32)
        for t in range(EXP_TG):
            n = sub * EXP_TG + t
            xn = h_ref[pl.ds(n, 1), :]
            prod = buf_ref[slot, pl.ds(t * PEER_SEL, PEER_SEL), pl.ds(0, d)] * xn
            part = prod[:, 0:128]
            for j in range(1, d // 128):
                part = part + prod[:, j * 128:(j + 1) * 128]
            s_col = jnp.sum(part, axis=-1, keepdims=True)
            act = jnp.where(lane == n, s_col, act)
        coeff = gt_ref[...] * _gelu_tanh(act)
        for t in range(EXP_TG):
            n = sub * EXP_TG + t
            c_col = jnp.sum(jnp.where(lane == n, coeff, 0.0), axis=-1, keepdims=True)
            wv = buf_ref[slot, pl.ds(t * PEER_SEL, PEER_SEL), pl.ds(d, d)] * c_col
            o_ref[pl.ds(n, 1), :] = x_ref[pl.ds(n, 1), :] + jnp.sum(wv, axis=0, keepdims=True)
        return carry

    lax.fori_loop(0, n_sub, sub_block, 0)


def peer_experts(x, gain, ids_flat, gates_t, uv):
    m, d = x.shape
    return pl.pallas_call(
        _expert_kernel,
        grid=(m // EXP_TB,),
        in_specs=[pl.BlockSpec((EXP_TB * PEER_SEL,), lambda i: (i,), memory_space=pltpu.SMEM),
                  pl.BlockSpec((EXP_TB, d), lambda i: (i, 0)),
                  pl.BlockSpec((1, d), lambda i: (0, 0)),
                  pl.BlockSpec((PEER_SEL, EXP_TB), lambda i: (0, i)),
                  pl.BlockSpec(memory_space=pl.ANY)],
        out_specs=pl.BlockSpec((EXP_TB, d), lambda i: (i, 0)),
        out_shape=jax.ShapeDtypeStruct((m, d), F32),
        scratch_shapes=[pltpu.VMEM((2, EXP_ROWS, 2 * d), F32),
                        pltpu.VMEM((EXP_TB, d), F32),
                        pltpu.SemaphoreType.DMA((2,))],
        compiler_params=_params("arbitrary"),
        name="peer_experts",
    )(ids_flat, x, gain.reshape(1, d), gates_t, uv)


def _relayout_w_in(w):
    d = w.shape[0]
    z = lambda n: jnp.zeros((d, n), w.dtype)
    return jnp.concatenate([
        w[:, 0:2048],
        w[:, 2056:4104],
        w[:, 4104:5128],
        w[:, 5640:6664],
        w[:, 5128:5640],
        w[:, 2048:2056], z(120),
        w[:, 6664:6680], z(112),
        z(256)], axis=1)


def kernel(x, norm_mix, norm_ffn, norm_final, w_in, w_out, gdn_conv_w, gdn_a_log, gdn_dt_bias, gdn_norm,
           hgrn_lb_logits, hgrn_norm, lru_conv_w, lru_conv_b, lru_w_a, lru_b_a, lru_w_x, lru_b_x, lru_lambda,
           gla_w_gate, gla_b_gate, gla_norm, peer_w_q, peer_sub_keys, peer_u, peer_v):
    bsz, t, d = x.shape
    m = bsz * t
    depth = w_in.shape[0]
    cc = _chunk_consts()
    lb_all = jnp.cumsum(jax.nn.softmax(hgrn_lb_logits.astype(F32), axis=0), axis=0)
    lb_all = lb_all - lb_all[0]
    tm = 512 if m % 512 == 0 else 256
    xf = x.reshape(m, d)
    for l in range(depth):
        w_in_l = _relayout_w_in(w_in[l]).astype(BF16)
        proj = norm_matmul(xf, norm_mix[l], w_in_l, tm, P_WIDTH // 4).reshape(bsz, t, P_WIDTH)
        y_a = gdn_mixer(proj, gdn_conv_w[l], gdn_a_log[l], gdn_dt_bias[l], gdn_norm[l], cc)
        y_b = hgrn_mixer(proj, lb_all[l], hgrn_norm[l], cc)
        y_c = lru_mixer(proj, lru_conv_w[l], lru_conv_b[l], lru_w_a[l], lru_b_a[l], lru_w_x[l], lru_b_x[l],
                        lru_lambda[l])
        y_d = gla_mixer(proj, gla_w_gate[l], gla_b_gate[l], gla_norm[l], cc)
        ys = [y.reshape(m, GROUP_W) for y in (y_a, y_b, y_c, y_d)]
        xf = out_proj(ys, w_out[l].reshape(4, GROUP_W, d).astype(BF16), xf, tm)
        sk = peer_sub_keys[l].reshape(PEER_HEADS * 2, PEER_NKEYS, -1).astype(BF16)
        experts_t, gates_t = peer_route(xf, norm_ffn[l], peer_w_q[l].astype(BF16), sk)
        ids_flat = experts_t.T.reshape(m * PEER_SEL)
        uv = jnp.concatenate([peer_u[l], peer_v[l]], axis=1)
        xf = peer_experts(xf, norm_ffn[l], ids_flat, gates_t, uv)
    return final_norm(xf, norm_final, tm).reshape(bsz, t, d)
```

```python
import functools
import math

import numpy as np
import jax
import jax.numpy as jnp
from jax import lax
from jax.experimental import pallas as pl
from jax.experimental.pallas import tpu as pltpu

F32 = jnp.float32
BF16 = jnp.bfloat16
EPS = 1e-6
CHUNK = 64
GROUP_W = 512
N_HEADS = 4
HEAD_DV = 128
GLA_DK = 64
GLA_RANK = 16
GLA_TAU = 16.0
LRU_C = 8.0
LRU_BLOCKS = 4
PEER_HEADS = 8
PEER_NKEYS = 128
PEER_TOPK = 16
PEER_SEL = PEER_HEADS * PEER_TOPK
NEG_BIG = -1e30
VMEM_LIMIT = 56 * 1024 * 1024

P_AQ, P_AK, P_AV, P_AZ, P_BQ, P_BF, P_BI, P_BG, P_CX, P_CG, P_DV, P_DG, P_DQK, P_SM = range(14)
P_BLOCKS = 14
P_WIDTH = P_BLOCKS * GROUP_W


def _dot(a, b):
    return jnp.dot(a, b, preferred_element_type=F32)


def _dot_nt(a, b):
    return lax.dot_general(a, b, (((1,), (1,)), ((), ())), preferred_element_type=F32)


def _split3(a):
    a1 = a.astype(BF16)
    r1 = a - a1.astype(F32)
    a2 = r1.astype(BF16)
    r2 = r1 - a2.astype(F32)
    return a1, a2, r2.astype(BF16)


def _mm3(a, b, nt=False):
    d = _dot_nt if nt else _dot
    a1, a2, _ = _split3(a)
    b1, b2, _ = _split3(b)
    return d(a1, b1) + (d(a1, b2) + d(a2, b1))


def _mm_exact_l(w_bf, x):
    x1, x2, x3 = _split3(x)
    return _dot(w_bf, x1) + (_dot(w_bf, x2) + _dot(w_bf, x3))


def _mm_exact_r(x, w_bf):
    x1, x2, x3 = _split3(x)
    return _dot(x1, w_bf) + (_dot(x2, w_bf) + _dot(x3, w_bf))


def _sigmoid(x):
    return 1.0 / (1.0 + jnp.exp(-x))


def _silu(x):
    return x * _sigmoid(x)


def _softplus(x):
    return jnp.maximum(x, 0.0) + jnp.log(1.0 + jnp.exp(-jnp.abs(x)))


def _gelu_tanh(x):
    return 0.5 * x * (1.0 + jnp.tanh(math.sqrt(2.0 / math.pi) * (x + 0.044715 * (x * x * x))))


def _rms(x, gain):
    return x * lax.rsqrt(jnp.mean(x * x, axis=-1, keepdims=True) + EPS) * gain


def _params(*sem):
    return pltpu.CompilerParams(dimension_semantics=sem, vmem_limit_bytes=VMEM_LIMIT)


def _norm_matmul_kernel(x_ref, g_ref, w_ref, o_ref, h_ref):
    @pl.when(pl.program_id(1) == 0)
    def _():
        h_ref[...] = _rms(x_ref[...], g_ref[...]).astype(BF16)

    o_ref[...] = _dot(h_ref[...], w_ref[...]).astype(o_ref.dtype)


def norm_matmul(x, gain, w_bf, tm, tn, out_dtype=F32):
    m, d = x.shape
    n = w_bf.shape[1]
    return pl.pallas_call(
        _norm_matmul_kernel,
        grid=(m // tm, n // tn),
        in_specs=[pl.BlockSpec((tm, d), lambda i, j: (i, 0)),
                  pl.BlockSpec((1, d), lambda i, j: (0, 0)),
                  pl.BlockSpec((d, tn), lambda i, j: (0, j))],
        out_specs=pl.BlockSpec((tm, tn), lambda i, j: (i, j)),
        out_shape=jax.ShapeDtypeStruct((m, n), out_dtype),
        scratch_shapes=[pltpu.VMEM((tm, d), BF16)],
        compiler_params=_params("parallel", "arbitrary"),
        name="norm_matmul",
    )(x, gain.reshape(1, d), w_bf)


def _out_proj_kernel(ya_ref, yb_ref, yc_ref, yd_ref, w_ref, x_ref, o_ref):
    acc = x_ref[...]
    for g, y_ref in enumerate((ya_ref, yb_ref, yc_ref, yd_ref)):
        acc = acc + _dot(y_ref[...], w_ref[g])
    o_ref[...] = acc


def out_proj(ys, w_bf, x, tm):
    m, d = x.shape
    y_spec = pl.BlockSpec((tm, GROUP_W), lambda i: (i, 0))
    return pl.pallas_call(
        _out_proj_kernel,
        grid=(m // tm,),
        in_specs=[y_spec, y_spec, y_spec, y_spec,
                  pl.BlockSpec((4, GROUP_W, d), lambda i: (0, 0, 0)),
                  pl.BlockSpec((tm, d), lambda i: (i, 0))],
        out_specs=pl.BlockSpec((tm, d), lambda i: (i, 0)),
        out_shape=jax.ShapeDtypeStruct((m, d), F32),
        compiler_params=_params("parallel"),
        name="out_proj",
    )(*ys, w_bf, x)


def _final_norm_kernel(x_ref, g_ref, o_ref):
    o_ref[...] = _rms(x_ref[...], g_ref[...])


def final_norm(x, gain, tm):
    m, d = x.shape
    return pl.pallas_call(
        _final_norm_kernel,
        grid=(m // tm,),
        in_specs=[pl.BlockSpec((tm, d), lambda i: (i, 0)), pl.BlockSpec((1, d), lambda i: (0, 0))],
        out_specs=pl.BlockSpec((tm, d), lambda i: (i, 0)),
        out_shape=jax.ShapeDtypeStruct((m, d), F32),
        compiler_params=_params("parallel"),
        name="final_norm",
    )(x, gain.reshape(1, d))


def _chunk_consts():
    i = np.arange(CHUNK)[:, None]
    t = np.arange(CHUNK)[None, :]
    tril = (t <= i).astype(np.float32)
    after = (t > i).astype(np.float32)
    levels, low, masks = [], [], []
    s = CHUNK // 2
    while s >= 1:
        pos = i % (2 * s)
        ref_row = (i // (2 * s)) * (2 * s) + s
        lower = pos >= s
        w = np.where(lower, (t > ref_row) & (t <= i), (t > i) & (t <= ref_row)).astype(np.float32)
        levels.append(w)
        low.append(np.broadcast_to(lower, (CHUNK, 128)).astype(np.float32))
        same = (i // (2 * s)) == (t // (2 * s))
        masks.append((same & lower & ((t % (2 * s)) < s)).astype(np.float32))
        s //= 2
    wall = np.concatenate([tril, after] + levels, axis=0)
    return (jnp.asarray(wall, BF16), jnp.asarray(np.stack(low), F32), jnp.asarray(np.stack(masks), F32),
            jnp.asarray(tril, BF16), jnp.asarray(tril.T, BF16))


N_LEVELS = int(math.log2(CHUNK))


def _gla_chunks(qs_all, k_all, v_all, lg_all, sts, dk, wall, low_ref, mask_ref):
    nh = len(sts)
    hs = lambda a, h, w: a[:, h * w:(h + 1) * w]
    e = _mm_exact_l(wall, lg_all)
    bcum = e[0:CHUNK]
    brev = e[CHUNK:2 * CHUNK]
    q_bf = qs_all.astype(BF16)
    k_bf = k_all.astype(BF16)
    eye = _eye()
    attn = [jnp.where(eye, _dot_nt(hs(q_bf, h, dk), hs(k_bf, h, dk)), 0.0) for h in range(nh)]
    for lv in range(N_LEVELS):
        f = jnp.exp(e[(2 + lv) * CHUNK:(3 + lv) * CHUNK])
        z = (jnp.where(low_ref[lv][:, 0:1] > 0.5, qs_all, k_all) * f).astype(BF16)
        keep = mask_ref[lv] > 0.5
        attn = [attn[h] + jnp.where(keep, _dot_nt(hs(z, h, dk), hs(z, h, dk)), 0.0) for h in range(nh)]
    qd = (qs_all * jnp.exp(bcum)).astype(BF16)
    kdec = (k_all * jnp.exp(brev)).astype(BF16)
    eb = jnp.exp(bcum[CHUNK - 1:CHUNK])
    v_bf = v_all.astype(BF16)
    outs = [_dot_nt(hs(qd, h, dk), sts[h].astype(BF16)) + _dot(attn[h].astype(BF16), hs(v_bf, h, HEAD_DV))
            for h in range(nh)]
    new = [sts[h] * hs(eb, h, dk) + _dot(hs(v_all, h, HEAD_DV).T.astype(BF16), hs(kdec, h, dk))
           for h in range(nh)]
    return outs, new


def _eye():
    r = lax.broadcasted_iota(jnp.int32, (CHUNK, CHUNK), 0)
    c = lax.broadcasted_iota(jnp.int32, (CHUNK, CHUNK), 1)
    return r == c


def _gated_out(o, gate, nw):
    o = o * lax.rsqrt(jnp.mean(o * o, axis=-1, keepdims=True) + EPS) * nw
    return o * _silu(gate)


def _hgrn_kernel(q_ref, f_ref, i_ref, g_ref, lb_ref, nw_ref, wall_ref, low_ref, mask_ref, o_ref, st_ref):
    @pl.when(pl.program_id(1) == 0)
    def _():
        st_ref[...] = jnp.zeros_like(st_ref)

    rows = lambda fn: jnp.concatenate([fn(bb) for bb in range(MIX_BB)], axis=1)
    lb = rows(lambda bb: lb_ref[...])
    sg = _sigmoid(rows(lambda bb: f_ref[bb]))
    lg_all = jnp.log(lb + (1.0 - lb) * sg)
    k_all = (1.0 - lb) * (1.0 - sg)
    q_all = _silu(rows(lambda bb: q_ref[bb])) * (HEAD_DV ** -0.5)
    v_all = rows(lambda bb: i_ref[bb])
    _gla_finish(q_all, k_all, v_all, lg_all, HEAD_DV, g_ref, nw_ref, wall_ref, low_ref, mask_ref, o_ref, st_ref)


def _gla_finish(q_all, k_all, v_all, lg_all, dk, g_ref, nw_ref, wall_ref, low_ref, mask_ref, o_ref, st_ref):
    pairs = [(bb, h) for bb in range(MIX_BB) for h in range(N_HEADS)]
    sts = [st_ref[bb, h] for bb, h in pairs]
    outs, new = _gla_chunks(q_all, k_all, v_all, lg_all, sts, dk, wall_ref[...], low_ref, mask_ref)
    for (bb, h), o, st_new in zip(pairs, outs, new):
        sl = slice(h * HEAD_DV, (h + 1) * HEAD_DV)
        st_ref[bb, h] = st_new
        o_ref[bb, :, sl] = _gated_out(o, g_ref[bb, :, sl], nw_ref[...]).astype(o_ref.dtype)


def _gla_kernel(qk_ref, v_ref, g_ref, sm_ref, wg_ref, bg_ref, nw_ref, wall_ref, low_ref, mask_ref, o_ref, st_ref):
    @pl.when(pl.program_id(1) == 0)
    def _():
        st_ref[...] = jnp.zeros_like(st_ref)

    rows = lambda fn: jnp.concatenate([fn(bb) for bb in range(MIX_BB)], axis=1)
    nq = N_HEADS * GLA_DK
    gate_in = rows(lambda bb: _mm3(sm_ref[bb], wg_ref[...]) + bg_ref[...])
    lg_all = -_softplus(-gate_in) * (1.0 / GLA_TAU)
    q_all = rows(lambda bb: qk_ref[bb, :, 0:nq]) * (GLA_DK ** -0.5)
    k_all = rows(lambda bb: qk_ref[bb, :, nq:2 * nq])
    v_all = rows(lambda bb: v_ref[bb])
    _gla_finish(q_all, k_all, v_all, lg_all, GLA_DK, g_ref, nw_ref, wall_ref, low_ref, mask_ref, o_ref, st_ref)


MIX_BB = 2


def _pspec(col_block, width=GROUP_W):
    return pl.BlockSpec((MIX_BB, CHUNK, width), lambda b, c: (b, c, col_block))


def _full(shape):
    nd = len(shape)
    return pl.BlockSpec(shape, lambda b, c: (0,) * nd)


def _per_batch_row(body, n_blocked, n_consts):
    def kern(*refs):
        blocked = refs[:n_blocked]
        consts = refs[n_blocked:n_blocked + n_consts]
        out = refs[n_blocked + n_consts]
        scratch = refs[n_blocked + n_consts + 1:]
        for bb in range(MIX_BB):
            body(*[r.at[bb] for r in blocked], *consts, out.at[bb], *[r.at[bb] for r in scratch])
    return kern


def _mixer_call(body, proj, col_specs, consts, scratch, name, per_row=True):
    bsz, t, _ = proj.shape
    return pl.pallas_call(
        _per_batch_row(body, len(col_specs), len(consts)) if per_row else body,
        grid=(bsz // MIX_BB, t // CHUNK),
        in_specs=col_specs + [_full(c.shape) for c in consts],
        out_specs=pl.BlockSpec((MIX_BB, CHUNK, GROUP_W), lambda b, c: (b, c, 0)),
        out_shape=jax.ShapeDtypeStruct((bsz, t, GROUP_W), BF16),
        scratch_shapes=[pltpu.VMEM((MIX_BB,) + shape, F32) for shape in scratch],
        compiler_params=_params("parallel", "arbitrary"),
        name=name,
    )(*([proj] * len(col_specs)), *consts)


def hgrn_mixer(proj, lb, nw, cc):
    wall, low, masks, _, _ = cc
    return _mixer_call(
        _hgrn_kernel, proj, [_pspec(P_BQ), _pspec(P_BF), _pspec(P_BI), _pspec(P_BG)],
        [lb.reshape(1, GROUP_W), nw.reshape(1, HEAD_DV), wall, low, masks],
        [(N_HEADS, HEAD_DV, HEAD_DV)], "hgrn2_mixer", per_row=False)


def gla_mixer(proj, w_gate, b_gate, nw, cc):
    wall, low, masks, _, _ = cc
    wg = jnp.zeros((128, N_HEADS * GLA_DK), F32).at[0:GLA_RANK].set(w_gate)
    return _mixer_call(
        _gla_kernel, proj, [_pspec(P_DQK), _pspec(P_DV), _pspec(P_DG), _pspec(P_SM * 4 + 1, 128)],
        [wg, b_gate.reshape(1, N_HEADS * GLA_DK), nw.reshape(1, HEAD_DV), wall, low, masks],
        [(N_HEADS, HEAD_DV, GLA_DK)], "gla_mixer", per_row=False)


def _causal_conv(x_ref, tail_ref, slot, cw):
    cur = x_ref[...]
    ext = jnp.concatenate([tail_ref[slot], cur], axis=0)
    y = ext[5:5 + CHUNK] * cw[0:1]
    for w in range(1, 4):
        y = y + ext[5 + w:5 + w + CHUNK] * cw[w:w + 1]
    tail_ref[slot] = cur[CHUNK - 8:CHUNK]
    return y


def _l2n(t):
    return t * lax.rsqrt(jnp.sum(t * t, axis=-1, keepdims=True) + EPS)


def _gdn_kernel(q_ref, k_ref, v_ref, z_ref, sm_ref, cw_ref, alog_ref, dtb_ref, nw_ref, tril_ref, trilt_ref,
                o_ref, tail_ref, s_ref):
    @pl.when(pl.program_id(1) == 0)
    def _():
        tail_ref[...] = jnp.zeros_like(tail_ref)
        s_ref[...] = jnp.zeros_like(s_ref)

    r = lax.broadcasted_iota(jnp.int32, (CHUNK, CHUNK), 0)
    c = lax.broadcasted_iota(jnp.int32, (CHUNK, CHUNK), 1)
    causal = r >= c
    strict = r > c
    eye = (r == c).astype(F32)
    scale = HEAD_DV ** -0.5
    pairs = [(bb, h) for bb in range(MIX_BB) for h in range(N_HEADS)]
    each = lambda fn: [fn(i) for i in range(len(pairs))]
    q_rows, k_rows, v_rows, gcum, gcum_t, beta_rows = [], [], [], [], [], []
    for bb in range(MIX_BB):
        tails = tail_ref.at[bb]
        q_rows.append(_silu(_causal_conv(q_ref.at[bb], tails, 0, cw_ref[0])))
        k_rows.append(_silu(_causal_conv(k_ref.at[bb], tails, 1, cw_ref[1])))
        v_rows.append(_silu(_causal_conv(v_ref.at[bb], tails, 2, cw_ref[2])))
        sm = sm_ref[bb]
        g_full = -(jnp.exp(alog_ref[...]) * _softplus(sm + dtb_ref[...]))
        beta_rows.append(_sigmoid(sm))
        gcum.append(_mm_exact_l(tril_ref[...], g_full))
        gcum_t.append(_mm_exact_r(g_full.T, trilt_ref[...]))
    head = lambda rows, i: rows[pairs[i][0]][:, pairs[i][1] * HEAD_DV:(pairs[i][1] + 1) * HEAD_DV]
    qh = each(lambda i: _l2n(head(q_rows, i)) * scale)
    kh = each(lambda i: _l2n(head(k_rows, i)))
    vh = each(lambda i: head(v_rows, i))
    gc = each(lambda i: gcum[pairs[i][0]][:, pairs[i][1]:pairs[i][1] + 1])
    gl = each(lambda i: gc[i][CHUNK - 1:CHUNK])
    beta = each(lambda i: beta_rows[pairs[i][0]][:, N_HEADS + pairs[i][1]:N_HEADS + pairs[i][1] + 1])
    decay = each(lambda i: jnp.exp(jnp.where(
        causal, gc[i] - gcum_t[pairs[i][0]][pairs[i][1]:pairs[i][1] + 1, :], NEG_BIG)))
    kb = each(lambda i: kh[i] * beta[i])
    x = each(lambda i: -jnp.where(strict, _mm3(kb[i], kh[i], nt=True) * decay[i], 0.0))
    tinv = each(lambda i: eye + x[i])
    p = x
    for _ in range(N_LEVELS - 1):
        p = each(lambda i: _mm3(p[i], p[i]))
        tinv = each(lambda i: tinv[i] + _mm3(tinv[i], p[i]))
    egc = each(lambda i: jnp.exp(gc[i]))
    u = each(lambda i: _mm3(tinv[i], vh[i] * beta[i]))
    w = each(lambda i: _mm3(tinv[i], kb[i] * egc[i]))
    s = each(lambda i: s_ref[pairs[i][0], pairs[i][1]])
    v_new = each(lambda i: u[i] - _mm3(w[i], s[i]))
    attn = each(lambda i: jnp.where(causal, _mm3(qh[i], kh[i], nt=True) * decay[i], 0.0))
    o = each(lambda i: _mm3(qh[i] * egc[i], s[i]) + _mm3(attn[i], v_new[i]))
    s_new = each(lambda i: s[i] * jnp.exp(gl[i]) + _mm3((kh[i] * jnp.exp(gl[i] - gc[i])).T, v_new[i]))
    for i, (bb, h) in enumerate(pairs):
        sl = slice(h * HEAD_DV, (h + 1) * HEAD_DV)
        s_ref[bb, h] = s_new[i]
        o_ref[bb, :, sl] = _gated_out(o[i], z_ref[bb, :, sl], nw_ref[...]).astype(o_ref.dtype)


def gdn_mixer(proj, conv_w, a_log, dt_bias, nw, cc):
    _, _, _, tril, trilt = cc
    cw = jnp.zeros((3, 8, GROUP_W), F32).at[:, 0:4].set(conv_w.reshape(4, 3, GROUP_W).transpose(1, 0, 2))
    alog = jnp.zeros((1, 128), F32).at[0, 0:N_HEADS].set(a_log)
    dtb = jnp.zeros((1, 128), F32).at[0, 0:N_HEADS].set(dt_bias)
    return _mixer_call(
        _gdn_kernel, proj,
        [_pspec(P_AQ), _pspec(P_AK), _pspec(P_AV), _pspec(P_AZ), _pspec(P_SM * 4, 128)],
        [cw, alog, dtb, nw.reshape(1, HEAD_DV), tril, trilt],
        [(3, 8, GROUP_W), (N_HEADS, HEAD_DV, HEAD_DV)], "gdn_mixer", per_row=False)


def _shift_rows(x, d, fill):
    rows = lax.broadcasted_iota(jnp.int32, x.shape, 0)
    return jnp.where(rows >= d, pltpu.roll(x, d, 0), fill)


def _lru_kernel(x_ref, g_ref, cw_ref, cb_ref, wa_ref, ba_ref, wx_ref, bx_ref, lam_ref, o_ref, tail_ref, h_ref):
    @pl.when(pl.program_id(1) == 0)
    def _():
        tail_ref[...] = jnp.zeros_like(tail_ref)
        h_ref[...] = jnp.zeros_like(h_ref)

    xb = _causal_conv(x_ref, tail_ref, 0, cw_ref[...]) + cb_ref[...]
    xb_bf = xb.astype(BF16)
    bw = GROUP_W // LRU_BLOCKS
    ra = jnp.concatenate([_dot(xb_bf[:, g * bw:(g + 1) * bw], wa_ref[g]) for g in range(LRU_BLOCKS)], axis=1)
    rx = jnp.concatenate([_dot(xb_bf[:, g * bw:(g + 1) * bw], wx_ref[g]) for g in range(LRU_BLOCKS)], axis=1)
    rg = _sigmoid(ra + ba_ref[...])
    ig = _sigmoid(rx + bx_ref[...])
    log_a = -LRU_C * rg * _softplus(-lam_ref[...])
    a = jnp.exp(log_a)
    b = jnp.sqrt(1.0 - jnp.exp(2.0 * log_a)) * ig * xb
    d = 1
    while d < CHUNK:
        b = a * _shift_rows(b, d, 0.0) + b
        a = a * _shift_rows(a, d, 1.0)
        d *= 2
    h = b + a * h_ref[...]
    h_ref[...] = h[CHUNK - 1:CHUNK]
    o_ref[...] = (h * _gelu_tanh(g_ref[...])).astype(o_ref.dtype)


def lru_mixer(proj, conv_w, conv_b, w_a, b_a, w_x, b_x, lam):
    cw = jnp.zeros((8, GROUP_W), F32).at[0:4].set(conv_w)
    row = lambda v: v.reshape(1, GROUP_W)
    return _mixer_call(
        _lru_kernel, proj, [_pspec(P_CX), _pspec(P_CG)],
        [cw, row(conv_b), w_a.astype(BF16), row(b_a), w_x.astype(BF16), row(b_x), row(lam)],
        [(1, 8, GROUP_W), (1, GROUP_W)], "rglru_mixer")


ROUTE_TB = 256


def _topk_rows(s, k):
    n = s.shape[0]
    rows = lax.broadcasted_iota(jnp.int32, s.shape, 0)
    vals, ids = [], []
    for _ in range(k):
        m = jnp.max(s, axis=0, keepdims=True)
        idx = jnp.min(jnp.where(s == m, rows, n), axis=0, keepdims=True)
        vals.append(m)
        ids.append(idx)
        s = jnp.where(rows == idx, NEG_BIG, s)
    return jnp.concatenate(vals, axis=0), jnp.concatenate(ids, axis=0)


def _route_kernel(x_ref, g_ref, wq_ref, sk_ref, ex_ref, gt_ref):
    h2 = _rms(x_ref[...], g_ref[...]).astype(BF16)
    q = _dot(h2, wq_ref[...])
    for h in range(PEER_HEADS):
        tops = []
        for p in range(2):
            col = (h * 2 + p) * PEER_NKEYS
            sc = _dot_nt(sk_ref[h * 2 + p], q[:, col:col + PEER_NKEYS].astype(BF16))
            tops.append(_topk_rows(sc, PEER_TOPK))
        (s1, i1), (s2, i2) = tops
        sub = lax.broadcasted_iota(jnp.int32, (8, s1.shape[1]), 0)
        half = PEER_TOPK // 2
        cs = [s1[0:1] + s2[0:half], s1[0:1] + s2[half:PEER_TOPK]]
        ci = [i1[0:1] * PEER_NKEYS + i2[0:half], i1[0:1] * PEER_NKEYS + i2[half:PEER_TOPK]]
        for a in range(1, half):
            n_valid = PEER_TOPK // (a + 1)
            cs.append(jnp.where(sub < n_valid, s1[a:a + 1] + s2[0:half], NEG_BIG))
            ci.append(i1[a:a + 1] * PEER_NKEYS + i2[0:half])
        cs.append(s1[half:PEER_TOPK] + s2[0:1])
        ci.append(i1[half:PEER_TOPK] * PEER_NKEYS + i2[0:1])
        cand_s = jnp.concatenate(cs, axis=0)
        cand_i = jnp.concatenate(ci, axis=0)
        best_s, best_pos = _topk_rows(cand_s, PEER_TOPK)
        pos = lax.broadcasted_iota(jnp.int32, cand_s.shape, 0)
        experts = jnp.concatenate(
            [jnp.sum(jnp.where(pos == best_pos[r:r + 1], cand_i, 0), axis=0, keepdims=True)
             for r in range(PEER_TOPK)], axis=0)
        ew = jnp.exp(best_s - best_s[0:1])
        gates = ew / jnp.sum(ew, axis=0, keepdims=True)
        ex_ref[h * PEER_TOPK:(h + 1) * PEER_TOPK, :] = experts
        gt_ref[h * PEER_TOPK:(h + 1) * PEER_TOPK, :] = gates


def peer_route(x, gain, wq_bf, sk_bf):
    m, d = x.shape
    nq = wq_bf.shape[1]
    return pl.pallas_call(
        _route_kernel,
        grid=(m // ROUTE_TB,),
        in_specs=[pl.BlockSpec((ROUTE_TB, d), lambda i: (i, 0)),
                  pl.BlockSpec((1, d), lambda i: (0, 0)),
                  pl.BlockSpec((d, nq), lambda i: (0, 0)),
                  pl.BlockSpec(sk_bf.shape, lambda i: (0, 0, 0))],
        out_specs=[pl.BlockSpec((PEER_SEL, ROUTE_TB), lambda i: (0, i)),
                   pl.BlockSpec((PEER_SEL, ROUTE_TB), lambda i: (0, i))],
        out_shape=[jax.ShapeDtypeStruct((PEER_SEL, m), jnp.int32),
                   jax.ShapeDtypeStruct((PEER_SEL, m), F32)],
        compiler_params=_params("parallel"),
        name="peer_route",
    )(x, gain.reshape(1, d), wq_bf, sk_bf)


EXP_TB = 64
GATE_W = 128
EXP_TG = 8
EXP_SLOTS = 4
EXP_AHEAD = 2
SUBLANES = 8
EXP_ROWS = EXP_TG * PEER_SEL
EXP_TILE_ROWS = EXP_ROWS // SUBLANES
TOK_TILE_ROWS = PEER_SEL // SUBLANES


def _expert_kernel(ids_ref, nxt_ref, x_ref, g_ref, gt_ref, uv_ref, o_ref, *scratch):
    bufs, (h_ref, sem) = scratch[:EXP_SLOTS], scratch[EXP_SLOTS:]
    d = x_ref.shape[1]
    n_col = d // 128
    n_sub = EXP_TB // EXP_TG
    step = pl.program_id(0)
    lane0 = (step % (GATE_W // EXP_TB)) * EXP_TB
    rows_per_iter = EXP_TILE_ROWS // (2 * EXP_TG)
    h_ref[...] = _rms(x_ref[...], g_ref[...])

    def issue_tile_row(src_ids, sub, slot, i):
        base = sub * EXP_ROWS + i * SUBLANES
        for k in range(SUBLANES):
            e = src_ids[base + k]
            pltpu.make_async_copy(uv_ref.at[e], bufs[slot].at[i, :, k, :], sem.at[slot]).start(priority=k % 2)

    def issue_all(src_ids, sub, slot):
        def body(i, carry):
            issue_tile_row(src_ids, sub, slot, i)
            return carry
        lax.fori_loop(0, EXP_TILE_ROWS, body, 0)

    def wait(slot):
        pltpu.make_async_copy(bufs[(slot + 1) % EXP_SLOTS], bufs[slot], sem.at[slot]).wait()

    lane = lax.broadcasted_iota(jnp.int32, (PEER_SEL, GATE_W), 1)

    def compute(sub, slot, ahead):
        def request(first_row):
            if ahead is not None:
                for ii in range(rows_per_iter):
                    issue_tile_row(ahead[0], ahead[1], ahead[2], first_row + ii)

        def u_side(t, act):
            n = sub * EXP_TG + t
            request(t * rows_per_iter)
            xn = h_ref[pl.ds(n, 1), :]
            parts = []
            for g in range(TOK_TILE_ROWS):
                i = t * TOK_TILE_ROWS + g
                acc = bufs[slot][i, 0] * xn[:, 0:128]
                for j in range(1, n_col):
                    acc = acc + bufs[slot][i, j] * xn[:, j * 128:(j + 1) * 128]
                parts.append(acc)
            s_col = jnp.sum(jnp.concatenate(parts, axis=0), axis=-1, keepdims=True)
            return jnp.where(lane == lane0 + n, s_col, act)

        act = lax.fori_loop(0, EXP_TG, u_side, jnp.zeros((PEER_SEL, GATE_W), F32))
        coeff = gt_ref[...] * _gelu_tanh(act)

        def v_side(t, carry):
            n = sub * EXP_TG + t
            request((EXP_TG + t) * rows_per_iter)
            c_col = jnp.sum(jnp.where(lane == lane0 + n, coeff, 0.0), axis=-1, keepdims=True)
            c_b = jnp.broadcast_to(c_col, (PEER_SEL, 128))
            outs = []
            for j in range(n_col):
                acc = bufs[slot][t * TOK_TILE_ROWS, n_col + j] * c_b[0:SUBLANES]
                for g in range(1, TOK_TILE_ROWS):
                    acc = acc + bufs[slot][t * TOK_TILE_ROWS + g, n_col + j] * c_b[g * SUBLANES:(g + 1) * SUBLANES]
                outs.append(jnp.sum(acc, axis=0, keepdims=True))
            o_ref[pl.ds(n, 1), :] = x_ref[pl.ds(n, 1), :] + jnp.concatenate(outs, axis=1)
            return carry

        lax.fori_loop(0, EXP_TG, v_side, 0)

    @pl.when(step == 0)
    def _():
        for s in range(EXP_AHEAD):
            issue_all(ids_ref, s, s)

    def group(q, carry):
        for r in range(EXP_SLOTS):
            sub = q * EXP_SLOTS + r
            wait(r)
            nxt_slot = (r + EXP_AHEAD) % EXP_SLOTS
            if r + EXP_AHEAD < EXP_SLOTS:
                compute(sub, r, (ids_ref, sub + EXP_AHEAD, nxt_slot))
            else:
                in_step = sub + EXP_AHEAD < n_sub
                pl.when(in_step)(functools.partial(compute, sub, r, (ids_ref, sub + EXP_AHEAD, nxt_slot)))
                pl.when(jnp.logical_not(in_step))(
                    functools.partial(compute, sub, r, (nxt_ref, r + EXP_AHEAD - EXP_SLOTS, nxt_slot)))
        return carry

    lax.fori_loop(0, n_sub // EXP_SLOTS, group, 0)

    @pl.when(step + 1 == pl.num_programs(0))
    def _():
        for s in range(EXP_AHEAD):
            wait(s)


def peer_experts(x, gain, ids_flat, gates_t, uv):
    m, d = x.shape
    n_blk = m // EXP_TB
    uv_tiles = uv.reshape(uv.shape[0], 2 * d // 128, 128)
    ids_spec = lambda index_map: pl.BlockSpec((EXP_TB * PEER_SEL,), index_map, memory_space=pltpu.SMEM)
    return pl.pallas_call(
        _expert_kernel,
        grid=(n_blk,),
        in_specs=[ids_spec(lambda i: (i,)),
                  ids_spec(lambda i: (jnp.minimum(i + 1, n_blk - 1),)),
                  pl.BlockSpec((EXP_TB, d), lambda i: (i, 0)),
                  pl.BlockSpec((1, d), lambda i: (0, 0)),
                  pl.BlockSpec((PEER_SEL, GATE_W), lambda i: (0, i // (GATE_W // EXP_TB))),
                  pl.BlockSpec(memory_space=pl.ANY)],
        out_specs=pl.BlockSpec((EXP_TB, d), lambda i: (i, 0)),
        out_shape=jax.ShapeDtypeStruct((m, d), F32),
        scratch_shapes=[pltpu.VMEM((EXP_TILE_ROWS, 2 * d // 128, SUBLANES, 128), F32)] * EXP_SLOTS + [
                        pltpu.VMEM((EXP_TB, d), F32),
                        pltpu.SemaphoreType.DMA((EXP_SLOTS,))],
        compiler_params=_params("arbitrary"),
        name="peer_experts",
    )(ids_flat, ids_flat, x, gain.reshape(1, d), gates_t, uv_tiles)


def _relayout_w_in(w):
    d = w.shape[0]
    z = lambda n: jnp.zeros((d, n), w.dtype)
    return jnp.concatenate([
        w[:, 0:2048],
        w[:, 2056:4104],
        w[:, 4104:5128],
        w[:, 5640:6664],
        w[:, 5128:5640],
        w[:, 2048:2056], z(120),
        w[:, 6664:6680], z(112),
        z(256)], axis=1)


def kernel(x, norm_mix, norm_ffn, norm_final, w_in, w_out, gdn_conv_w, gdn_a_log, gdn_dt_bias, gdn_norm,
           hgrn_lb_logits, hgrn_norm, lru_conv_w, lru_conv_b, lru_w_a, lru_b_a, lru_w_x, lru_b_x, lru_lambda,
           gla_w_gate, gla_b_gate, gla_norm, peer_w_q, peer_sub_keys, peer_u, peer_v):
    bsz, t, d = x.shape
    m = bsz * t
    depth = w_in.shape[0]
    cc = _chunk_consts()
    lb_all = jnp.cumsum(jax.nn.softmax(hgrn_lb_logits.astype(F32), axis=0), axis=0)
    lb_all = lb_all - lb_all[0]
    tm = 512 if m % 512 == 0 else 256
    xf = x.reshape(m, d)
    for l in range(depth):
        w_in_l = _relayout_w_in(w_in[l]).astype(BF16)
        proj = norm_matmul(xf, norm_mix[l], w_in_l, tm, P_WIDTH // 4).reshape(bsz, t, P_WIDTH)
        y_a = gdn_mixer(proj, gdn_conv_w[l], gdn_a_log[l], gdn_dt_bias[l], gdn_norm[l], cc)
        y_b = hgrn_mixer(proj, lb_all[l], hgrn_norm[l], cc)
        y_c = lru_mixer(proj, lru_conv_w[l], lru_conv_b[l], lru_w_a[l], lru_b_a[l], lru_w_x[l], lru_b_x[l],
                        lru_lambda[l])
        y_d = gla_mixer(proj, gla_w_gate[l], gla_b_gate[l], gla_norm[l], cc)
        ys = [y.reshape(m, GROUP_W) for y in (y_a, y_b, y_c, y_d)]
        xf = out_proj(ys, w_out[l].reshape(4, GROUP_W, d).astype(BF16), xf, tm)
        sk = peer_sub_keys[l].reshape(PEER_HEADS * 2, PEER_NKEYS, -1).astype(BF16)
        experts_t, gates_t = peer_route(xf, norm_ffn[l], peer_w_q[l].astype(BF16), sk)
        ids_flat = experts_t.T.reshape(m * PEER_SEL)
        uv = jnp.concatenate([peer_u[l], peer_v[l]], axis=1)
        xf = peer_experts(xf, norm_ffn[l], ids_flat, gates_t, uv)
    return final_norm(xf, norm_final, tm).reshape(bsz, t, d)
```

```python
import functools
import math

import numpy as np
import jax
import jax.numpy as jnp
from jax import lax
from jax.experimental import pallas as pl
from jax.experimental.pallas import tpu as pltpu

F32 = jnp.float32
BF16 = jnp.bfloat16
EPS = 1e-6
CHUNK = 64
GROUP_W = 512
N_HEADS = 4
HEAD_DV = 128
GLA_DK = 64
GLA_RANK = 16
GLA_TAU = 16.0
LRU_C = 8.0
LRU_BLOCKS = 4
PEER_HEADS = 8
PEER_NKEYS = 128
PEER_TOPK = 16
PEER_SEL = PEER_HEADS * PEER_TOPK
NEG_BIG = -1e30
VMEM_LIMIT = 56 * 1024 * 1024

P_AQ, P_AK, P_AV, P_AZ, P_BQ, P_BF, P_BI, P_BG, P_CX, P_CG, P_DV, P_DG, P_DQK, P_SM = range(14)
P_BLOCKS = 14
P_WIDTH = P_BLOCKS * GROUP_W


def _dot(a, b):
    return jnp.dot(a, b, preferred_element_type=F32)


def _dot_nt(a, b):
    return lax.dot_general(a, b, (((1,), (1,)), ((), ())), preferred_element_type=F32)


def _split3(a):
    a1 = a.astype(BF16)
    r1 = a - a1.astype(F32)
    a2 = r1.astype(BF16)
    r2 = r1 - a2.astype(F32)
    return a1, a2, r2.astype(BF16)


def _mm3(a, b, nt=False):
    d = _dot_nt if nt else _dot
    a1, a2, _ = _split3(a)
    b1, b2, _ = _split3(b)
    return d(a1, b1) + (d(a1, b2) + d(a2, b1))


def _mm_exact_l(w_bf, x):
    x1, x2, x3 = _split3(x)
    return _dot(w_bf, x1) + (_dot(w_bf, x2) + _dot(w_bf, x3))


def _mm_exact_r(x, w_bf):
    x1, x2, x3 = _split3(x)
    return _dot(x1, w_bf) + (_dot(x2, w_bf) + _dot(x3, w_bf))


def _sigmoid(x):
    return 1.0 / (1.0 + jnp.exp(-x))


def _silu(x):
    return x * _sigmoid(x)


def _softplus(x):
    return jnp.maximum(x, 0.0) + jnp.log(1.0 + jnp.exp(-jnp.abs(x)))


def _gelu_tanh(x):
    return 0.5 * x * (1.0 + jnp.tanh(math.sqrt(2.0 / math.pi) * (x + 0.044715 * (x * x * x))))


def _rms(x, gain):
    return x * lax.rsqrt(jnp.mean(x * x, axis=-1, keepdims=True) + EPS) * gain


def _params(*sem):
    return pltpu.CompilerParams(dimension_semantics=sem, vmem_limit_bytes=VMEM_LIMIT)


def _norm_matmul_kernel(x_ref, g_ref, w_ref, o_ref, h_ref):
    @pl.when(pl.program_id(1) == 0)
    def _():
        h_ref[...] = _rms(x_ref[...], g_ref[...]).astype(BF16)

    o_ref[...] = _dot(h_ref[...], w_ref[...]).astype(o_ref.dtype)


def norm_matmul(x, gain, w_bf, tm, tn, out_dtype=F32):
    m, d = x.shape
    n = w_bf.shape[1]
    return pl.pallas_call(
        _norm_matmul_kernel,
        grid=(m // tm, n // tn),
        in_specs=[pl.BlockSpec((tm, d), lambda i, j: (i, 0)),
                  pl.BlockSpec((1, d), lambda i, j: (0, 0)),
                  pl.BlockSpec((d, tn), lambda i, j: (0, j))],
        out_specs=pl.BlockSpec((tm, tn), lambda i, j: (i, j)),
        out_shape=jax.ShapeDtypeStruct((m, n), out_dtype),
        scratch_shapes=[pltpu.VMEM((tm, d), BF16)],
        compiler_params=_params("parallel", "arbitrary"),
        name="norm_matmul",
    )(x, gain.reshape(1, d), w_bf)


def _out_proj_kernel(ya_ref, yb_ref, yc_ref, yd_ref, w_ref, x_ref, o_ref):
    acc = x_ref[...]
    for g, y_ref in enumerate((ya_ref, yb_ref, yc_ref, yd_ref)):
        acc = acc + _dot(y_ref[...], w_ref[g])
    o_ref[...] = acc


def out_proj(ys, w_bf, x, tm):
    m, d = x.shape
    y_spec = pl.BlockSpec((tm, GROUP_W), lambda i: (i, 0))
    return pl.pallas_call(
        _out_proj_kernel,
        grid=(m // tm,),
        in_specs=[y_spec, y_spec, y_spec, y_spec,
                  pl.BlockSpec((4, GROUP_W, d), lambda i: (0, 0, 0)),
                  pl.BlockSpec((tm, d), lambda i: (i, 0))],
        out_specs=pl.BlockSpec((tm, d), lambda i: (i, 0)),
        out_shape=jax.ShapeDtypeStruct((m, d), F32),
        compiler_params=_params("parallel"),
        name="out_proj",
    )(*ys, w_bf, x)


def _final_norm_kernel(x_ref, g_ref, o_ref):
    o_ref[...] = _rms(x_ref[...], g_ref[...])


def final_norm(x, gain, tm):
    m, d = x.shape
    return pl.pallas_call(
        _final_norm_kernel,
        grid=(m // tm,),
        in_specs=[pl.BlockSpec((tm, d), lambda i: (i, 0)), pl.BlockSpec((1, d), lambda i: (0, 0))],
        out_specs=pl.BlockSpec((tm, d), lambda i: (i, 0)),
        out_shape=jax.ShapeDtypeStruct((m, d), F32),
        compiler_params=_params("parallel"),
        name="final_norm",
    )(x, gain.reshape(1, d))


def _chunk_consts():
    i = np.arange(CHUNK)[:, None]
    t = np.arange(CHUNK)[None, :]
    tril = (t <= i).astype(np.float32)
    after = (t > i).astype(np.float32)
    levels, low, masks = [], [], []
    s = CHUNK // 2
    while s >= 1:
        pos = i % (2 * s)
        ref_row = (i // (2 * s)) * (2 * s) + s
        lower = pos >= s
        w = np.where(lower, (t > ref_row) & (t <= i), (t > i) & (t <= ref_row)).astype(np.float32)
        levels.append(w)
        low.append(np.broadcast_to(lower, (CHUNK, 128)).astype(np.float32))
        same = (i // (2 * s)) == (t // (2 * s))
        masks.append((same & lower & ((t % (2 * s)) < s)).astype(np.float32))
        s //= 2
    wall = np.concatenate([tril, after] + levels, axis=0)
    return (jnp.asarray(wall, BF16), jnp.asarray(np.stack(low), F32), jnp.asarray(np.stack(masks), F32),
            jnp.asarray(tril, BF16), jnp.asarray(tril.T, BF16))


N_LEVELS = int(math.log2(CHUNK))


def _gla_chunks(qs_all, k_all, v_all, lg_all, sts, dk, wall, low_ref, mask_ref):
    nh = len(sts)
    hs = lambda a, h, w: a[:, h * w:(h + 1) * w]
    e = _mm_exact_l(wall, lg_all)
    bcum = e[0:CHUNK]
    brev = e[CHUNK:2 * CHUNK]
    q_bf = qs_all.astype(BF16)
    k_bf = k_all.astype(BF16)
    eye = _eye()
    attn = [jnp.where(eye, _dot_nt(hs(q_bf, h, dk), hs(k_bf, h, dk)), 0.0) for h in range(nh)]
    for lv in range(N_LEVELS):
        f = jnp.exp(e[(2 + lv) * CHUNK:(3 + lv) * CHUNK])
        z = (jnp.where(low_ref[lv][:, 0:1] > 0.5, qs_all, k_all) * f).astype(BF16)
        keep = mask_ref[lv] > 0.5
        attn = [attn[h] + jnp.where(keep, _dot_nt(hs(z, h, dk), hs(z, h, dk)), 0.0) for h in range(nh)]
    qd = (qs_all * jnp.exp(bcum)).astype(BF16)
    kdec = (k_all * jnp.exp(brev)).astype(BF16)
    eb = jnp.exp(bcum[CHUNK - 1:CHUNK])
    v_bf = v_all.astype(BF16)
    outs = [_dot_nt(hs(qd, h, dk), sts[h].astype(BF16)) + _dot(attn[h].astype(BF16), hs(v_bf, h, HEAD_DV))
            for h in range(nh)]
    new = [sts[h] * hs(eb, h, dk) + _dot(hs(v_all, h, HEAD_DV).T.astype(BF16), hs(kdec, h, dk))
           for h in range(nh)]
    return outs, new


def _eye():
    r = lax.broadcasted_iota(jnp.int32, (CHUNK, CHUNK), 0)
    c = lax.broadcasted_iota(jnp.int32, (CHUNK, CHUNK), 1)
    return r == c


def _gated_out(o, gate, nw):
    o = o * lax.rsqrt(jnp.mean(o * o, axis=-1, keepdims=True) + EPS) * nw
    return o * _silu(gate)


def _hgrn_kernel(q_ref, f_ref, i_ref, g_ref, lb_ref, nw_ref, wall_ref, low_ref, mask_ref, o_ref, st_ref):
    @pl.when(pl.program_id(1) == 0)
    def _():
        st_ref[...] = jnp.zeros_like(st_ref)

    rows = lambda fn: jnp.concatenate([fn(bb) for bb in range(MIX_BB)], axis=1)
    lb = rows(lambda bb: lb_ref[...])
    sg = _sigmoid(rows(lambda bb: f_ref[bb]))
    lg_all = jnp.log(lb + (1.0 - lb) * sg)
    k_all = (1.0 - lb) * (1.0 - sg)
    q_all = _silu(rows(lambda bb: q_ref[bb])) * (HEAD_DV ** -0.5)
    v_all = rows(lambda bb: i_ref[bb])
    _gla_finish(q_all, k_all, v_all, lg_all, HEAD_DV, g_ref, nw_ref, wall_ref, low_ref, mask_ref, o_ref, st_ref)


def _gla_finish(q_all, k_all, v_all, lg_all, dk, g_ref, nw_ref, wall_ref, low_ref, mask_ref, o_ref, st_ref):
    pairs = [(bb, h) for bb in range(MIX_BB) for h in range(N_HEADS)]
    sts = [st_ref[bb, h] for bb, h in pairs]
    outs, new = _gla_chunks(q_all, k_all, v_all, lg_all, sts, dk, wall_ref[...], low_ref, mask_ref)
    for (bb, h), o, st_new in zip(pairs, outs, new):
        sl = slice(h * HEAD_DV, (h + 1) * HEAD_DV)
        st_ref[bb, h] = st_new
        o_ref[bb, :, sl] = _gated_out(o, g_ref[bb, :, sl], nw_ref[...]).astype(o_ref.dtype)


def _gla_kernel(qk_ref, v_ref, g_ref, sm_ref, wg_ref, bg_ref, nw_ref, wall_ref, low_ref, mask_ref, o_ref, st_ref):
    @pl.when(pl.program_id(1) == 0)
    def _():
        st_ref[...] = jnp.zeros_like(st_ref)

    rows = lambda fn: jnp.concatenate([fn(bb) for bb in range(MIX_BB)], axis=1)
    nq = N_HEADS * GLA_DK
    gate_in = rows(lambda bb: _mm3(sm_ref[bb], wg_ref[...]) + bg_ref[...])
    lg_all = -_softplus(-gate_in) * (1.0 / GLA_TAU)
    q_all = rows(lambda bb: qk_ref[bb, :, 0:nq]) * (GLA_DK ** -0.5)
    k_all = rows(lambda bb: qk_ref[bb, :, nq:2 * nq])
    v_all = rows(lambda bb: v_ref[bb])
    _gla_finish(q_all, k_all, v_all, lg_all, GLA_DK, g_ref, nw_ref, wall_ref, low_ref, mask_ref, o_ref, st_ref)


MIX_BB = 2


def _pspec(col_block, width=GROUP_W):
    return pl.BlockSpec((MIX_BB, CHUNK, width), lambda b, c: (b, c, col_block))


def _full(shape):
    nd = len(shape)
    return pl.BlockSpec(shape, lambda b, c: (0,) * nd)


def _per_batch_row(body, n_blocked, n_consts):
    def kern(*refs):
        blocked = refs[:n_blocked]
        consts = refs[n_blocked:n_blocked + n_consts]
        out = refs[n_blocked + n_consts]
        scratch = refs[n_blocked + n_consts + 1:]
        for bb in range(MIX_BB):
            body(*[r.at[bb] for r in blocked], *consts, out.at[bb], *[r.at[bb] for r in scratch])
    return kern


def _mixer_call(body, proj, col_specs, consts, scratch, name, per_row=True):
    bsz, t, _ = proj.shape
    return pl.pallas_call(
        _per_batch_row(body, len(col_specs), len(consts)) if per_row else body,
        grid=(bsz // MIX_BB, t // CHUNK),
        in_specs=col_specs + [_full(c.shape) for c in consts],
        out_specs=pl.BlockSpec((MIX_BB, CHUNK, GROUP_W), lambda b, c: (b, c, 0)),
        out_shape=jax.ShapeDtypeStruct((bsz, t, GROUP_W), BF16),
        scratch_shapes=[pltpu.VMEM((MIX_BB,) + shape, F32) for shape in scratch],
        compiler_params=_params("parallel", "arbitrary"),
        name=name,
    )(*([proj] * len(col_specs)), *consts)


def hgrn_mixer(proj, lb, nw, cc):
    wall, low, masks, _, _ = cc
    return _mixer_call(
        _hgrn_kernel, proj, [_pspec(P_BQ), _pspec(P_BF), _pspec(P_BI), _pspec(P_BG)],
        [lb.reshape(1, GROUP_W), nw.reshape(1, HEAD_DV), wall, low, masks],
        [(N_HEADS, HEAD_DV, HEAD_DV)], "hgrn2_mixer", per_row=False)


def gla_mixer(proj, w_gate, b_gate, nw, cc):
    wall, low, masks, _, _ = cc
    wg = jnp.zeros((128, N_HEADS * GLA_DK), F32).at[0:GLA_RANK].set(w_gate)
    return _mixer_call(
        _gla_kernel, proj, [_pspec(P_DQK), _pspec(P_DV), _pspec(P_DG), _pspec(P_SM * 4 + 1, 128)],
        [wg, b_gate.reshape(1, N_HEADS * GLA_DK), nw.reshape(1, HEAD_DV), wall, low, masks],
        [(N_HEADS, HEAD_DV, GLA_DK)], "gla_mixer", per_row=False)


def _causal_conv(x_ref, tail_ref, slot, cw):
    cur = x_ref[...]
    ext = jnp.concatenate([tail_ref[slot], cur], axis=0)
    y = ext[5:5 + CHUNK] * cw[0:1]
    for w in range(1, 4):
        y = y + ext[5 + w:5 + w + CHUNK] * cw[w:w + 1]
    tail_ref[slot] = cur[CHUNK - 8:CHUNK]
    return y


def _l2n(t):
    return t * lax.rsqrt(jnp.sum(t * t, axis=-1, keepdims=True) + EPS)


def _gdn_kernel(q_ref, k_ref, v_ref, z_ref, sm_ref, cw_ref, alog_ref, dtb_ref, nw_ref, tril_ref, trilt_ref,
                o_ref, tail_ref, s_ref):
    @pl.when(pl.program_id(1) == 0)
    def _():
        tail_ref[...] = jnp.zeros_like(tail_ref)
        s_ref[...] = jnp.zeros_like(s_ref)

    r = lax.broadcasted_iota(jnp.int32, (CHUNK, CHUNK), 0)
    c = lax.broadcasted_iota(jnp.int32, (CHUNK, CHUNK), 1)
    causal = r >= c
    strict = r > c
    eye = (r == c).astype(F32)
    scale = HEAD_DV ** -0.5
    pairs = [(bb, h) for bb in range(MIX_BB) for h in range(N_HEADS)]
    each = lambda fn: [fn(i) for i in range(len(pairs))]
    q_rows, k_rows, v_rows, gcum, gcum_t, beta_rows = [], [], [], [], [], []
    for bb in range(MIX_BB):
        tails = tail_ref.at[bb]
        q_rows.append(_silu(_causal_conv(q_ref.at[bb], tails, 0, cw_ref[0])))
        k_rows.append(_silu(_causal_conv(k_ref.at[bb], tails, 1, cw_ref[1])))
        v_rows.append(_silu(_causal_conv(v_ref.at[bb], tails, 2, cw_ref[2])))
        sm = sm_ref[bb]
        g_full = -(jnp.exp(alog_ref[...]) * _softplus(sm + dtb_ref[...]))
        beta_rows.append(_sigmoid(sm))
        gcum.append(_mm_exact_l(tril_ref[...], g_full))
        gcum_t.append(_mm_exact_r(g_full.T, trilt_ref[...]))
    head = lambda rows, i: rows[pairs[i][0]][:, pairs[i][1] * HEAD_DV:(pairs[i][1] + 1) * HEAD_DV]
    qh = each(lambda i: _l2n(head(q_rows, i)) * scale)
    kh = each(lambda i: _l2n(head(k_rows, i)))
    vh = each(lambda i: head(v_rows, i))
    gc = each(lambda i: gcum[pairs[i][0]][:, pairs[i][1]:pairs[i][1] + 1])
    gl = each(lambda i: gc[i][CHUNK - 1:CHUNK])
    beta = each(lambda i: beta_rows[pairs[i][0]][:, N_HEADS + pairs[i][1]:N_HEADS + pairs[i][1] + 1])
    decay = each(lambda i: jnp.exp(jnp.where(
        causal, gc[i] - gcum_t[pairs[i][0]][pairs[i][1]:pairs[i][1] + 1, :], NEG_BIG)))
    kb = each(lambda i: kh[i] * beta[i])
    x = each(lambda i: -jnp.where(strict, _mm3(kb[i], kh[i], nt=True) * decay[i], 0.0))
    tinv = each(lambda i: eye + x[i])
    p = x
    for _ in range(N_LEVELS - 1):
        p = each(lambda i: _mm3(p[i], p[i]))
        tinv = each(lambda i: tinv[i] + _mm3(tinv[i], p[i]))
    egc = each(lambda i: jnp.exp(gc[i]))
    u = each(lambda i: _mm3(tinv[i], vh[i] * beta[i]))
    w = each(lambda i: _mm3(tinv[i], kb[i] * egc[i]))
    s = each(lambda i: s_ref[pairs[i][0], pairs[i][1]])
    v_new = each(lambda i: u[i] - _mm3(w[i], s[i]))
    attn = each(lambda i: jnp.where(causal, _mm3(qh[i], kh[i], nt=True) * decay[i], 0.0))
    o = each(lambda i: _mm3(qh[i] * egc[i], s[i]) + _mm3(attn[i], v_new[i]))
    s_new = each(lambda i: s[i] * jnp.exp(gl[i]) + _mm3((kh[i] * jnp.exp(gl[i] - gc[i])).T, v_new[i]))
    for i, (bb, h) in enumerate(pairs):
        sl = slice(h * HEAD_DV, (h + 1) * HEAD_DV)
        s_ref[bb, h] = s_new[i]
        o_ref[bb, :, sl] = _gated_out(o[i], z_ref[bb, :, sl], nw_ref[...]).astype(o_ref.dtype)


def gdn_mixer(proj, conv_w, a_log, dt_bias, nw, cc):
    _, _, _, tril, trilt = cc
    cw = jnp.zeros((3, 8, GROUP_W), F32).at[:, 0:4].set(conv_w.reshape(4, 3, GROUP_W).transpose(1, 0, 2))
    alog = jnp.zeros((1, 128), F32).at[0, 0:N_HEADS].set(a_log)
    dtb = jnp.zeros((1, 128), F32).at[0, 0:N_HEADS].set(dt_bias)
    return _mixer_call(
        _gdn_kernel, proj,
        [_pspec(P_AQ), _pspec(P_AK), _pspec(P_AV), _pspec(P_AZ), _pspec(P_SM * 4, 128)],
        [cw, alog, dtb, nw.reshape(1, HEAD_DV), tril, trilt],
        [(3, 8, GROUP_W), (N_HEADS, HEAD_DV, HEAD_DV)], "gdn_mixer", per_row=False)


def _shift_rows(x, d, fill):
    rows = lax.broadcasted_iota(jnp.int32, x.shape, 0)
    return jnp.where(rows >= d, pltpu.roll(x, d, 0), fill)


def _lru_kernel(x_ref, g_ref, cw_ref, cb_ref, wa_ref, ba_ref, wx_ref, bx_ref, lam_ref, o_ref, tail_ref, h_ref):
    @pl.when(pl.program_id(1) == 0)
    def _():
        tail_ref[...] = jnp.zeros_like(tail_ref)
        h_ref[...] = jnp.zeros_like(h_ref)

    xb = _causal_conv(x_ref, tail_ref, 0, cw_ref[...]) + cb_ref[...]
    xb_bf = xb.astype(BF16)
    bw = GROUP_W // LRU_BLOCKS
    ra = jnp.concatenate([_dot(xb_bf[:, g * bw:(g + 1) * bw], wa_ref[g]) for g in range(LRU_BLOCKS)], axis=1)
    rx = jnp.concatenate([_dot(xb_bf[:, g * bw:(g + 1) * bw], wx_ref[g]) for g in range(LRU_BLOCKS)], axis=1)
    rg = _sigmoid(ra + ba_ref[...])
    ig = _sigmoid(rx + bx_ref[...])
    log_a = -LRU_C * rg * _softplus(-lam_ref[...])
    a = jnp.exp(log_a)
    b = jnp.sqrt(1.0 - jnp.exp(2.0 * log_a)) * ig * xb
    d = 1
    while d < CHUNK:
        b = a * _shift_rows(b, d, 0.0) + b
        a = a * _shift_rows(a, d, 1.0)
        d *= 2
    h = b + a * h_ref[...]
    h_ref[...] = h[CHUNK - 1:CHUNK]
    o_ref[...] = (h * _gelu_tanh(g_ref[...])).astype(o_ref.dtype)


def lru_mixer(proj, conv_w, conv_b, w_a, b_a, w_x, b_x, lam):
    cw = jnp.zeros((8, GROUP_W), F32).at[0:4].set(conv_w)
    row = lambda v: v.reshape(1, GROUP_W)
    return _mixer_call(
        _lru_kernel, proj, [_pspec(P_CX), _pspec(P_CG)],
        [cw, row(conv_b), w_a.astype(BF16), row(b_a), w_x.astype(BF16), row(b_x), row(lam)],
        [(1, 8, GROUP_W), (1, GROUP_W)], "rglru_mixer")


ROUTE_TB = 256


def _topk_rows(s, k):
    n = s.shape[0]
    rows = lax.broadcasted_iota(jnp.int32, s.shape, 0)
    vals, ids = [], []
    for _ in range(k):
        m = jnp.max(s, axis=0, keepdims=True)
        idx = jnp.min(jnp.where(s == m, rows, n), axis=0, keepdims=True)
        vals.append(m)
        ids.append(idx)
        s = jnp.where(rows == idx, NEG_BIG, s)
    return jnp.concatenate(vals, axis=0), jnp.concatenate(ids, axis=0)


def _route_kernel(x_ref, g_ref, wq_ref, sk_ref, ex_ref, gt_ref):
    h2 = _rms(x_ref[...], g_ref[...]).astype(BF16)
    q = _dot(h2, wq_ref[...])
    for h in range(PEER_HEADS):
        tops = []
        for p in range(2):
            col = (h * 2 + p) * PEER_NKEYS
            sc = _dot_nt(sk_ref[h * 2 + p], q[:, col:col + PEER_NKEYS].astype(BF16))
            tops.append(_topk_rows(sc, PEER_TOPK))
        (s1, i1), (s2, i2) = tops
        sub = lax.broadcasted_iota(jnp.int32, (8, s1.shape[1]), 0)
        half = PEER_TOPK // 2
        cs = [s1[0:1] + s2[0:half], s1[0:1] + s2[half:PEER_TOPK]]
        ci = [i1[0:1] * PEER_NKEYS + i2[0:half], i1[0:1] * PEER_NKEYS + i2[half:PEER_TOPK]]
        for a in range(1, half):
            n_valid = PEER_TOPK // (a + 1)
            cs.append(jnp.where(sub < n_valid, s1[a:a + 1] + s2[0:half], NEG_BIG))
            ci.append(i1[a:a + 1] * PEER_NKEYS + i2[0:half])
        cs.append(s1[half:PEER_TOPK] + s2[0:1])
        ci.append(i1[half:PEER_TOPK] * PEER_NKEYS + i2[0:1])
        cand_s = jnp.concatenate(cs, axis=0)
        cand_i = jnp.concatenate(ci, axis=0)
        best_s, best_pos = _topk_rows(cand_s, PEER_TOPK)
        pos = lax.broadcasted_iota(jnp.int32, cand_s.shape, 0)
        experts = jnp.concatenate(
            [jnp.sum(jnp.where(pos == best_pos[r:r + 1], cand_i, 0), axis=0, keepdims=True)
             for r in range(PEER_TOPK)], axis=0)
        ew = jnp.exp(best_s - best_s[0:1])
        gates = ew / jnp.sum(ew, axis=0, keepdims=True)
        ex_ref[h * PEER_TOPK:(h + 1) * PEER_TOPK, :] = experts
        gt_ref[h * PEER_TOPK:(h + 1) * PEER_TOPK, :] = gates


def peer_route(x, gain, wq_bf, sk_bf):
    m, d = x.shape
    nq = wq_bf.shape[1]
    return pl.pallas_call(
        _route_kernel,
        grid=(m // ROUTE_TB,),
        in_specs=[pl.BlockSpec((ROUTE_TB, d), lambda i: (i, 0)),
                  pl.BlockSpec((1, d), lambda i: (0, 0)),
                  pl.BlockSpec((d, nq), lambda i: (0, 0)),
                  pl.BlockSpec(sk_bf.shape, lambda i: (0, 0, 0))],
        out_specs=[pl.BlockSpec((PEER_SEL, ROUTE_TB), lambda i: (0, i)),
                   pl.BlockSpec((PEER_SEL, ROUTE_TB), lambda i: (0, i))],
        out_shape=[jax.ShapeDtypeStruct((PEER_SEL, m), jnp.int32),
                   jax.ShapeDtypeStruct((PEER_SEL, m), F32)],
        compiler_params=_params("parallel"),
        name="peer_route",
    )(x, gain.reshape(1, d), wq_bf, sk_bf)


EXP_TB = 64
GATE_W = 128
EXP_TG = 8
EXP_SLOTS = 4
EXP_AHEAD = 2
SUBLANES = 8
EXP_ROWS = EXP_TG * PEER_SEL
EXP_TILE_ROWS = EXP_ROWS // SUBLANES
TOK_TILE_ROWS = PEER_SEL // SUBLANES


def _expert_kernel(ids_ref, nxt_ref, x_ref, g_ref, gt_ref, uv_ref, o_ref, *scratch):
    bufs, (h_ref, sem) = scratch[:EXP_SLOTS], scratch[EXP_SLOTS:]
    d = x_ref.shape[1]
    n_word = d // 256
    n_sub = EXP_TB // EXP_TG
    step = pl.program_id(0)
    lane0 = (step % (GATE_W // EXP_TB)) * EXP_TB
    rows_per_iter = EXP_TILE_ROWS // (2 * EXP_TG)
    h_ref[...] = _rms(x_ref[...], g_ref[...])

    def issue_tile_row(src_ids, sub, slot, i):
        base = sub * EXP_ROWS + i * SUBLANES
        for k in range(SUBLANES):
            e = src_ids[base + k]
            pltpu.make_async_copy(uv_ref.at[e], bufs[slot].at[i, :, k, :], sem.at[slot]).start(priority=k % 2)

    def issue_all(src_ids, sub, slot):
        def body(i, carry):
            issue_tile_row(src_ids, sub, slot, i)
            return carry
        lax.fori_loop(0, EXP_TILE_ROWS, body, 0)

    def wait(slot):
        pltpu.make_async_copy(bufs[(slot + 1) % EXP_SLOTS], bufs[slot], sem.at[slot]).wait()

    lane = lax.broadcasted_iota(jnp.int32, (PEER_SEL, GATE_W), 1)

    def compute(sub, slot, ahead):
        def request(first_row):
            if ahead is not None:
                for ii in range(rows_per_iter):
                    issue_tile_row(ahead[0], ahead[1], ahead[2], first_row + ii)

        def u_side(t, act):
            n = sub * EXP_TG + t
            request(t * rows_per_iter)
            xn = h_ref[pl.ds(n, 1), :]
            parts = []
            for g in range(TOK_TILE_ROWS):
                acc = None
                for c in range(n_word):
                    lo, hi = _unpack_pair(bufs[slot][t * TOK_TILE_ROWS + g, c])
                    term = (lo * xn[:, c * 128:(c + 1) * 128]
                            + hi * xn[:, d // 2 + c * 128:d // 2 + (c + 1) * 128])
                    acc = term if acc is None else acc + term
                parts.append(acc)
            s_col = jnp.sum(jnp.concatenate(parts, axis=0), axis=-1, keepdims=True)
            return jnp.where(lane == lane0 + n, s_col, act)

        act = lax.fori_loop(0, EXP_TG, u_side, jnp.zeros((PEER_SEL, GATE_W), F32))
        coeff = gt_ref[...] * _gelu_tanh(act)

        def v_side(t, carry):
            n = sub * EXP_TG + t
            request((EXP_TG + t) * rows_per_iter)
            c_col = jnp.sum(jnp.where(lane == lane0 + n, coeff, 0.0), axis=-1, keepdims=True)
            c_b = jnp.broadcast_to(c_col, (PEER_SEL, 128))
            outs_lo, outs_hi = [], []
            for c in range(n_word):
                acc_lo = acc_hi = None
                for g in range(TOK_TILE_ROWS):
                    lo, hi = _unpack_pair(bufs[slot][t * TOK_TILE_ROWS + g, n_word + c])
                    cg = c_b[g * SUBLANES:(g + 1) * SUBLANES]
                    acc_lo = lo * cg if acc_lo is None else acc_lo + lo * cg
                    acc_hi = hi * cg if acc_hi is None else acc_hi + hi * cg
                outs_lo.append(jnp.sum(acc_lo, axis=0, keepdims=True))
                outs_hi.append(jnp.sum(acc_hi, axis=0, keepdims=True))
            o_ref[pl.ds(n, 1), :] = x_ref[pl.ds(n, 1), :] + jnp.concatenate(outs_lo + outs_hi, axis=1)
            return carry

        lax.fori_loop(0, EXP_TG, v_side, 0)

    @pl.when(step == 0)
    def _():
        for s in range(EXP_AHEAD):
            issue_all(ids_ref, s, s)

    def group(q, carry):
        for r in range(EXP_SLOTS):
            sub = q * EXP_SLOTS + r
            wait(r)
            nxt_slot = (r + EXP_AHEAD) % EXP_SLOTS
            if r + EXP_AHEAD < EXP_SLOTS:
                compute(sub, r, (ids_ref, sub + EXP_AHEAD, nxt_slot))
            else:
                in_step = sub + EXP_AHEAD < n_sub
                pl.when(in_step)(functools.partial(compute, sub, r, (ids_ref, sub + EXP_AHEAD, nxt_slot)))
                pl.when(jnp.logical_not(in_step))(
                    functools.partial(compute, sub, r, (nxt_ref, r + EXP_AHEAD - EXP_SLOTS, nxt_slot)))
        return carry

    lax.fori_loop(0, n_sub // EXP_SLOTS, group, 0)

    @pl.when(step + 1 == pl.num_programs(0))
    def _():
        for s in range(EXP_AHEAD):
            wait(s)


def _pack_bf16_pairs(t):
    e, d = t.shape
    bits = lax.bitcast_convert_type(t.astype(BF16), jnp.uint16).astype(jnp.uint32)
    words = bits[:, :d // 2] | (bits[:, d // 2:] << 16)
    return words.reshape(e, d // 256, 128)


def _unpack_pair(w):
    lo = lax.bitcast_convert_type(w << 16, F32)
    hi = lax.bitcast_convert_type(w & jnp.uint32(0xFFFF0000), F32)
    return lo, hi


def peer_experts(x, gain, ids_flat, gates_t, u, v):
    m, d = x.shape
    n_blk = m // EXP_TB
    uv_tiles = jnp.concatenate([_pack_bf16_pairs(u), _pack_bf16_pairs(v)], axis=1)
    ids_spec = lambda index_map: pl.BlockSpec((EXP_TB * PEER_SEL,), index_map, memory_space=pltpu.SMEM)
    return pl.pallas_call(
        _expert_kernel,
        grid=(n_blk,),
        in_specs=[ids_spec(lambda i: (i,)),
                  ids_spec(lambda i: (jnp.minimum(i + 1, n_blk - 1),)),
                  pl.BlockSpec((EXP_TB, d), lambda i: (i, 0)),
                  pl.BlockSpec((1, d), lambda i: (0, 0)),
                  pl.BlockSpec((PEER_SEL, GATE_W), lambda i: (0, i // (GATE_W // EXP_TB))),
                  pl.BlockSpec(memory_space=pl.ANY)],
        out_specs=pl.BlockSpec((EXP_TB, d), lambda i: (i, 0)),
        out_shape=jax.ShapeDtypeStruct((m, d), F32),
        scratch_shapes=[pltpu.VMEM((EXP_TILE_ROWS, 2 * d // 256, SUBLANES, 128), jnp.uint32)] * EXP_SLOTS + [
                        pltpu.VMEM((EXP_TB, d), F32),
                        pltpu.SemaphoreType.DMA((EXP_SLOTS,))],
        compiler_params=_params("arbitrary"),
        name="peer_experts",
    )(ids_flat, ids_flat, x, gain.reshape(1, d), gates_t, uv_tiles)


def _relayout_w_in(w):
    d = w.shape[0]
    z = lambda n: jnp.zeros((d, n), w.dtype)
    return jnp.concatenate([
        w[:, 0:2048],
        w[:, 2056:4104],
        w[:, 4104:5128],
        w[:, 5640:6664],
        w[:, 5128:5640],
        w[:, 2048:2056], z(120),
        w[:, 6664:6680], z(112),
        z(256)], axis=1)


def kernel(x, norm_mix, norm_ffn, norm_final, w_in, w_out, gdn_conv_w, gdn_a_log, gdn_dt_bias, gdn_norm,
           hgrn_lb_logits, hgrn_norm, lru_conv_w, lru_conv_b, lru_w_a, lru_b_a, lru_w_x, lru_b_x, lru_lambda,
           gla_w_gate, gla_b_gate, gla_norm, peer_w_q, peer_sub_keys, peer_u, peer_v):
    bsz, t, d = x.shape
    m = bsz * t
    depth = w_in.shape[0]
    cc = _chunk_consts()
    lb_all = jnp.cumsum(jax.nn.softmax(hgrn_lb_logits.astype(F32), axis=0), axis=0)
    lb_all = lb_all - lb_all[0]
    tm = 512 if m % 512 == 0 else 256
    xf = x.reshape(m, d)
    for l in range(depth):
        w_in_l = _relayout_w_in(w_in[l]).astype(BF16)
        tm_in = 2 * tm if m % (2 * tm) == 0 else tm
        proj = norm_matmul(xf, norm_mix[l], w_in_l, tm_in, P_WIDTH // 4).reshape(bsz, t, P_WIDTH)
        y_a = gdn_mixer(proj, gdn_conv_w[l], gdn_a_log[l], gdn_dt_bias[l], gdn_norm[l], cc)
        y_b = hgrn_mixer(proj, lb_all[l], hgrn_norm[l], cc)
        y_c = lru_mixer(proj, lru_conv_w[l], lru_conv_b[l], lru_w_a[l], lru_b_a[l], lru_w_x[l], lru_b_x[l],
                        lru_lambda[l])
        y_d = gla_mixer(proj, gla_w_gate[l], gla_b_gate[l], gla_norm[l], cc)
        ys = [y.reshape(m, GROUP_W) for y in (y_a, y_b, y_c, y_d)]
        xf = out_proj(ys, w_out[l].reshape(4, GROUP_W, d).astype(BF16), xf, tm)
        sk = peer_sub_keys[l].reshape(PEER_HEADS * 2, PEER_NKEYS, -1).astype(BF16)
        experts_t, gates_t = peer_route(xf, norm_ffn[l], peer_w_q[l].astype(BF16), sk)
        ids_flat = experts_t.T.reshape(m * PEER_SEL)
        xf = peer_experts(xf, norm_ffn[l], ids_flat, gates_t, peer_u[l], peer_v[l])
    return final_norm(xf, norm_final, tm).reshape(bsz, t, d)
```

```python
import functools
import math

import numpy as np
import jax
import jax.numpy as jnp
from jax import lax
from jax.experimental import pallas as pl
from jax.experimental.pallas import tpu as pltpu

F32 = jnp.float32
BF16 = jnp.bfloat16
EPS = 1e-6
CHUNK = 64
GROUP_W = 512
N_HEADS = 4
HEAD_DV = 128
GLA_DK = 64
GLA_RANK = 16
GLA_TAU = 16.0
LRU_C = 8.0
LRU_BLOCKS = 4
PEER_HEADS = 8
PEER_NKEYS = 128
PEER_TOPK = 16
PEER_SEL = PEER_HEADS * PEER_TOPK
NEG_BIG = -1e30
VMEM_LIMIT = 56 * 1024 * 1024

P_AQ, P_AK, P_AV, P_AZ, P_BQ, P_BF, P_BI, P_BG, P_CX, P_CG, P_DV, P_DG, P_DQK, P_SM = range(14)
P_BLOCKS = 14
P_WIDTH = P_BLOCKS * GROUP_W


def _dot(a, b):
    return jnp.dot(a, b, preferred_element_type=F32)


def _dot_nt(a, b):
    return lax.dot_general(a, b, (((1,), (1,)), ((), ())), preferred_element_type=F32)


def _split3(a):
    a1 = a.astype(BF16)
    r1 = a - a1.astype(F32)
    a2 = r1.astype(BF16)
    r2 = r1 - a2.astype(F32)
    return a1, a2, r2.astype(BF16)


def _mm3(a, b, nt=False):
    d = _dot_nt if nt else _dot
    a1, a2, _ = _split3(a)
    b1, b2, _ = _split3(b)
    return d(a1, b1) + (d(a1, b2) + d(a2, b1))


def _mm_exact_l(w_bf, x):
    x1, x2, x3 = _split3(x)
    return _dot(w_bf, x1) + (_dot(w_bf, x2) + _dot(w_bf, x3))


def _mm_exact_r(x, w_bf):
    x1, x2, x3 = _split3(x)
    return _dot(x1, w_bf) + (_dot(x2, w_bf) + _dot(x3, w_bf))


def _sigmoid(x):
    return 1.0 / (1.0 + jnp.exp(-x))


def _silu(x):
    return x * _sigmoid(x)


def _softplus(x):
    return jnp.maximum(x, 0.0) + jnp.log(1.0 + jnp.exp(-jnp.abs(x)))


def _gelu_tanh(x):
    return 0.5 * x * (1.0 + jnp.tanh(math.sqrt(2.0 / math.pi) * (x + 0.044715 * (x * x * x))))


def _rms(x, gain):
    return x * lax.rsqrt(jnp.mean(x * x, axis=-1, keepdims=True) + EPS) * gain


def _params(*sem):
    return pltpu.CompilerParams(dimension_semantics=sem, vmem_limit_bytes=VMEM_LIMIT)


def _norm_matmul_kernel(x_ref, g_ref, w_ref, o_ref, h_ref):
    @pl.when(pl.program_id(1) == 0)
    def _():
        h_ref[...] = _rms(x_ref[...], g_ref[...]).astype(BF16)

    o_ref[...] = _dot(h_ref[...], w_ref[...]).astype(o_ref.dtype)


def norm_matmul(x, gain, w_bf, tm, tn, out_dtype=F32):
    m, d = x.shape
    n = w_bf.shape[1]
    return pl.pallas_call(
        _norm_matmul_kernel,
        grid=(m // tm, n // tn),
        in_specs=[pl.BlockSpec((tm, d), lambda i, j: (i, 0)),
                  pl.BlockSpec((1, d), lambda i, j: (0, 0)),
                  pl.BlockSpec((d, tn), lambda i, j: (0, j))],
        out_specs=pl.BlockSpec((tm, tn), lambda i, j: (i, j)),
        out_shape=jax.ShapeDtypeStruct((m, n), out_dtype),
        scratch_shapes=[pltpu.VMEM((tm, d), BF16)],
        compiler_params=_params("parallel", "arbitrary"),
        name="norm_matmul",
    )(x, gain.reshape(1, d), w_bf)


def _out_proj_kernel(ya_ref, yb_ref, yc_ref, yd_ref, w_ref, x_ref, o_ref):
    acc = x_ref[...]
    for g, y_ref in enumerate((ya_ref, yb_ref, yc_ref, yd_ref)):
        acc = acc + _dot(y_ref[...], w_ref[g])
    o_ref[...] = acc


def out_proj(ys, w_bf, x, tm):
    m, d = x.shape
    y_spec = pl.BlockSpec((tm, GROUP_W), lambda i: (i, 0))
    return pl.pallas_call(
        _out_proj_kernel,
        grid=(m // tm,),
        in_specs=[y_spec, y_spec, y_spec, y_spec,
                  pl.BlockSpec((4, GROUP_W, d), lambda i: (0, 0, 0)),
                  pl.BlockSpec((tm, d), lambda i: (i, 0))],
        out_specs=pl.BlockSpec((tm, d), lambda i: (i, 0)),
        out_shape=jax.ShapeDtypeStruct((m, d), F32),
        compiler_params=_params("parallel"),
        name="out_proj",
    )(*ys, w_bf, x)


def _chunk_consts():
    i = np.arange(CHUNK)[:, None]
    t = np.arange(CHUNK)[None, :]
    tril = (t <= i).astype(np.float32)
    after = (t > i).astype(np.float32)
    levels, low, masks = [], [], []
    s = CHUNK // 2
    while s >= 1:
        pos = i % (2 * s)
        ref_row = (i // (2 * s)) * (2 * s) + s
        lower = pos >= s
        w = np.where(lower, (t > ref_row) & (t <= i), (t > i) & (t <= ref_row)).astype(np.float32)
        levels.append(w)
        low.append(np.broadcast_to(lower, (CHUNK, 128)).astype(np.float32))
        same = (i // (2 * s)) == (t // (2 * s))
        masks.append((same & lower & ((t % (2 * s)) < s)).astype(np.float32))
        s //= 2
    wall = np.concatenate([tril, after] + levels, axis=0)
    return (jnp.asarray(wall, BF16), jnp.asarray(np.stack(low), F32), jnp.asarray(np.stack(masks), F32),
            jnp.asarray(tril, BF16), jnp.asarray(tril.T, BF16))


N_LEVELS = int(math.log2(CHUNK))


def _gla_chunks(qs_all, k_all, v_all, lg_all, sts, dk, wall, low_ref, mask_ref):
    nh = len(sts)
    hs = lambda a, h, w: a[:, h * w:(h + 1) * w]
    e = _mm_exact_l(wall, lg_all)
    bcum = e[0:CHUNK]
    brev = e[CHUNK:2 * CHUNK]
    q_bf = qs_all.astype(BF16)
    k_bf = k_all.astype(BF16)
    eye = _eye()
    attn = [jnp.where(eye, _dot_nt(hs(q_bf, h, dk), hs(k_bf, h, dk)), 0.0) for h in range(nh)]
    for lv in range(N_LEVELS):
        f = jnp.exp(e[(2 + lv) * CHUNK:(3 + lv) * CHUNK])
        z = (jnp.where(low_ref[lv][:, 0:1] > 0.5, qs_all, k_all) * f).astype(BF16)
        keep = mask_ref[lv] > 0.5
        attn = [attn[h] + jnp.where(keep, _dot_nt(hs(z, h, dk), hs(z, h, dk)), 0.0) for h in range(nh)]
    qd = (qs_all * jnp.exp(bcum)).astype(BF16)
    kdec = (k_all * jnp.exp(brev)).astype(BF16)
    eb = jnp.exp(bcum[CHUNK - 1:CHUNK])
    v_bf = v_all.astype(BF16)
    outs = [_dot_nt(hs(qd, h, dk), sts[h].astype(BF16)) + _dot(attn[h].astype(BF16), hs(v_bf, h, HEAD_DV))
            for h in range(nh)]
    new = [sts[h] * hs(eb, h, dk) + _dot(hs(v_all, h, HEAD_DV).T.astype(BF16), hs(kdec, h, dk))
           for h in range(nh)]
    return outs, new


def _eye():
    r = lax.broadcasted_iota(jnp.int32, (CHUNK, CHUNK), 0)
    c = lax.broadcasted_iota(jnp.int32, (CHUNK, CHUNK), 1)
    return r == c


def _gated_out(o, gate, nw):
    o = o * lax.rsqrt(jnp.mean(o * o, axis=-1, keepdims=True) + EPS) * nw
    return o * _silu(gate)


def _hgrn_kernel(q_ref, f_ref, i_ref, g_ref, lb_ref, nw_ref, wall_ref, low_ref, mask_ref, o_ref, st_ref):
    @pl.when(pl.program_id(1) == 0)
    def _():
        st_ref[...] = jnp.zeros_like(st_ref)

    rows = lambda fn: jnp.concatenate([fn(bb) for bb in range(MIX_BB)], axis=1)
    lb = rows(lambda bb: lb_ref[...])
    sg = _sigmoid(rows(lambda bb: f_ref[bb]))
    lg_all = jnp.log(lb + (1.0 - lb) * sg)
    k_all = (1.0 - lb) * (1.0 - sg)
    q_all = _silu(rows(lambda bb: q_ref[bb])) * (HEAD_DV ** -0.5)
    v_all = rows(lambda bb: i_ref[bb])
    _gla_finish(q_all, k_all, v_all, lg_all, HEAD_DV, g_ref, nw_ref, wall_ref, low_ref, mask_ref, o_ref, st_ref)


def _gla_finish(q_all, k_all, v_all, lg_all, dk, g_ref, nw_ref, wall_ref, low_ref, mask_ref, o_ref, st_ref):
    pairs = [(bb, h) for bb in range(MIX_BB) for h in range(N_HEADS)]
    sts = [st_ref[bb, h] for bb, h in pairs]
    outs, new = _gla_chunks(q_all, k_all, v_all, lg_all, sts, dk, wall_ref[...], low_ref, mask_ref)
    for (bb, h), o, st_new in zip(pairs, outs, new):
        sl = slice(h * HEAD_DV, (h + 1) * HEAD_DV)
        st_ref[bb, h] = st_new
        o_ref[bb, :, sl] = _gated_out(o, g_ref[bb, :, sl], nw_ref[...]).astype(o_ref.dtype)


def _gla_kernel(qk_ref, v_ref, g_ref, sm_ref, wg_ref, bg_ref, nw_ref, wall_ref, low_ref, mask_ref, o_ref, st_ref):
    @pl.when(pl.program_id(1) == 0)
    def _():
        st_ref[...] = jnp.zeros_like(st_ref)

    rows = lambda fn: jnp.concatenate([fn(bb) for bb in range(MIX_BB)], axis=1)
    nq = N_HEADS * GLA_DK
    gate_in = rows(lambda bb: _mm3(sm_ref[bb], wg_ref[...]) + bg_ref[...])
    lg_all = -_softplus(-gate_in) * (1.0 / GLA_TAU)
    q_all = rows(lambda bb: qk_ref[bb, :, 0:nq]) * (GLA_DK ** -0.5)
    k_all = rows(lambda bb: qk_ref[bb, :, nq:2 * nq])
    v_all = rows(lambda bb: v_ref[bb])
    _gla_finish(q_all, k_all, v_all, lg_all, GLA_DK, g_ref, nw_ref, wall_ref, low_ref, mask_ref, o_ref, st_ref)


MIX_BB = 4


def _pspec(col_block, width=GROUP_W):
    return pl.BlockSpec((MIX_BB, CHUNK, width), lambda b, c: (b, c, col_block))


def _full(shape):
    nd = len(shape)
    return pl.BlockSpec(shape, lambda b, c: (0,) * nd)


def _per_batch_row(body, n_blocked, n_consts):
    def kern(*refs):
        blocked = refs[:n_blocked]
        consts = refs[n_blocked:n_blocked + n_consts]
        out = refs[n_blocked + n_consts]
        scratch = refs[n_blocked + n_consts + 1:]
        for bb in range(MIX_BB):
            body(*[r.at[bb] for r in blocked], *consts, out.at[bb], *[r.at[bb] for r in scratch])
    return kern


def _mixer_call(body, proj, col_specs, consts, scratch, name, per_row=True):
    bsz, t, _ = proj.shape
    assert bsz % MIX_BB == 0 and t % CHUNK == 0, (bsz, t)
    return pl.pallas_call(
        _per_batch_row(body, len(col_specs), len(consts)) if per_row else body,
        grid=(bsz // MIX_BB, t // CHUNK),
        in_specs=col_specs + [_full(c.shape) for c in consts],
        out_specs=pl.BlockSpec((MIX_BB, CHUNK, GROUP_W), lambda b, c: (b, c, 0)),
        out_shape=jax.ShapeDtypeStruct((bsz, t, GROUP_W), BF16),
        scratch_shapes=[pltpu.VMEM((MIX_BB,) + shape, F32) for shape in scratch],
        compiler_params=_params("parallel", "arbitrary"),
        name=name,
    )(*([proj] * len(col_specs)), *consts)


def hgrn_mixer(proj, lb, nw, cc):
    wall, low, masks, _, _ = cc
    return _mixer_call(
        _hgrn_kernel, proj, [_pspec(P_BQ), _pspec(P_BF), _pspec(P_BI), _pspec(P_BG)],
        [lb.reshape(1, GROUP_W), nw.reshape(1, HEAD_DV), wall, low, masks],
        [(N_HEADS, HEAD_DV, HEAD_DV)], "hgrn2_mixer", per_row=False)


def gla_mixer(proj, w_gate, b_gate, nw, cc):
    wall, low, masks, _, _ = cc
    wg = jnp.zeros((128, N_HEADS * GLA_DK), F32).at[0:GLA_RANK].set(w_gate)
    return _mixer_call(
        _gla_kernel, proj, [_pspec(P_DQK), _pspec(P_DV), _pspec(P_DG), _pspec(P_SM * 4 + 1, 128)],
        [wg, b_gate.reshape(1, N_HEADS * GLA_DK), nw.reshape(1, HEAD_DV), wall, low, masks],
        [(N_HEADS, HEAD_DV, GLA_DK)], "gla_mixer", per_row=False)


def _causal_conv(x_ref, tail_ref, slot, cw):
    cur = x_ref[...]
    ext = jnp.concatenate([tail_ref[slot], cur], axis=0)
    y = ext[5:5 + CHUNK] * cw[0:1]
    for w in range(1, 4):
        y = y + ext[5 + w:5 + w + CHUNK] * cw[w:w + 1]
    tail_ref[slot] = cur[CHUNK - 8:CHUNK]
    return y


def _l2n(t):
    return t * lax.rsqrt(jnp.sum(t * t, axis=-1, keepdims=True) + EPS)


def _gdn_kernel(q_ref, k_ref, v_ref, z_ref, sm_ref, cw_ref, alog_ref, dtb_ref, nw_ref, tril_ref, trilt_ref,
                o_ref, tail_ref, s_ref):
    @pl.when(pl.program_id(1) == 0)
    def _():
        tail_ref[...] = jnp.zeros_like(tail_ref)
        s_ref[...] = jnp.zeros_like(s_ref)

    r = lax.broadcasted_iota(jnp.int32, (CHUNK, CHUNK), 0)
    c = lax.broadcasted_iota(jnp.int32, (CHUNK, CHUNK), 1)
    causal = r >= c
    strict = r > c
    eye = (r == c).astype(F32)
    scale = HEAD_DV ** -0.5
    pairs = [(bb, h) for bb in range(MIX_BB) for h in range(N_HEADS)]
    each = lambda fn: [fn(i) for i in range(len(pairs))]
    q_rows, k_rows, v_rows, gcum, gcum_t, beta_rows = [], [], [], [], [], []
    for bb in range(MIX_BB):
        tails = tail_ref.at[bb]
        q_rows.append(_silu(_causal_conv(q_ref.at[bb], tails, 0, cw_ref[0])))
        k_rows.append(_silu(_causal_conv(k_ref.at[bb], tails, 1, cw_ref[1])))
        v_rows.append(_silu(_causal_conv(v_ref.at[bb], tails, 2, cw_ref[2])))
        sm = sm_ref[bb]
        g_full = -(jnp.exp(alog_ref[...]) * _softplus(sm + dtb_ref[...]))
        beta_rows.append(_sigmoid(sm))
        gcum.append(_mm_exact_l(tril_ref[...], g_full))
        gcum_t.append(_mm_exact_r(g_full.T, trilt_ref[...]))
    head = lambda rows, i: rows[pairs[i][0]][:, pairs[i][1] * HEAD_DV:(pairs[i][1] + 1) * HEAD_DV]
    qh = each(lambda i: _l2n(head(q_rows, i)) * scale)
    kh = each(lambda i: _l2n(head(k_rows, i)))
    vh = each(lambda i: head(v_rows, i))
    gc = each(lambda i: gcum[pairs[i][0]][:, pairs[i][1]:pairs[i][1] + 1])
    gl = each(lambda i: gc[i][CHUNK - 1:CHUNK])
    beta = each(lambda i: beta_rows[pairs[i][0]][:, N_HEADS + pairs[i][1]:N_HEADS + pairs[i][1] + 1])
    decay = each(lambda i: jnp.exp(jnp.where(
        causal, gc[i] - gcum_t[pairs[i][0]][pairs[i][1]:pairs[i][1] + 1, :], NEG_BIG)))
    kb = each(lambda i: kh[i] * beta[i])
    x = each(lambda i: -jnp.where(strict, _mm3(kb[i], kh[i], nt=True) * decay[i], 0.0))
    tinv = each(lambda i: eye + x[i])
    p = x
    for _ in range(N_LEVELS - 1):
        p = each(lambda i: _mm3(p[i], p[i]))
        tinv = each(lambda i: tinv[i] + _mm3(tinv[i], p[i]))
    egc = each(lambda i: jnp.exp(gc[i]))
    u = each(lambda i: _mm3(tinv[i], vh[i] * beta[i]))
    w = each(lambda i: _mm3(tinv[i], kb[i] * egc[i]))
    s = each(lambda i: s_ref[pairs[i][0], pairs[i][1]])
    v_new = each(lambda i: u[i] - _mm3(w[i], s[i]))
    attn = each(lambda i: jnp.where(causal, _mm3(qh[i], kh[i], nt=True) * decay[i], 0.0))
    o = each(lambda i: _mm3(qh[i] * egc[i], s[i]) + _mm3(attn[i], v_new[i]))
    s_new = each(lambda i: s[i] * jnp.exp(gl[i]) + _mm3((kh[i] * jnp.exp(gl[i] - gc[i])).T, v_new[i]))
    for i, (bb, h) in enumerate(pairs):
        sl = slice(h * HEAD_DV, (h + 1) * HEAD_DV)
        s_ref[bb, h] = s_new[i]
        o_ref[bb, :, sl] = _gated_out(o[i], z_ref[bb, :, sl], nw_ref[...]).astype(o_ref.dtype)


def gdn_mixer(proj, conv_w, a_log, dt_bias, nw, cc):
    _, _, _, tril, trilt = cc
    cw = jnp.zeros((3, 8, GROUP_W), F32).at[:, 0:4].set(conv_w.reshape(4, 3, GROUP_W).transpose(1, 0, 2))
    alog = jnp.zeros((1, 128), F32).at[0, 0:N_HEADS].set(a_log)
    dtb = jnp.zeros((1, 128), F32).at[0, 0:N_HEADS].set(dt_bias)
    return _mixer_call(
        _gdn_kernel, proj,
        [_pspec(P_AQ), _pspec(P_AK), _pspec(P_AV), _pspec(P_AZ), _pspec(P_SM * 4, 128)],
        [cw, alog, dtb, nw.reshape(1, HEAD_DV), tril, trilt],
        [(3, 8, GROUP_W), (N_HEADS, HEAD_DV, HEAD_DV)], "gdn_mixer", per_row=False)


def _shift_rows(x, d, fill):
    rows = lax.broadcasted_iota(jnp.int32, x.shape, 0)
    return jnp.where(rows >= d, pltpu.roll(x, d, 0), fill)


def _lru_kernel(x_ref, g_ref, cw_ref, cb_ref, wa_ref, ba_ref, wx_ref, bx_ref, lam_ref, o_ref, tail_ref, h_ref):
    @pl.when(pl.program_id(1) == 0)
    def _():
        tail_ref[...] = jnp.zeros_like(tail_ref)
        h_ref[...] = jnp.zeros_like(h_ref)

    xb = _causal_conv(x_ref, tail_ref, 0, cw_ref[...]) + cb_ref[...]
    xb_bf = xb.astype(BF16)
    bw = GROUP_W // LRU_BLOCKS
    ra = jnp.concatenate([_dot(xb_bf[:, g * bw:(g + 1) * bw], wa_ref[g]) for g in range(LRU_BLOCKS)], axis=1)
    rx = jnp.concatenate([_dot(xb_bf[:, g * bw:(g + 1) * bw], wx_ref[g]) for g in range(LRU_BLOCKS)], axis=1)
    rg = _sigmoid(ra + ba_ref[...])
    ig = _sigmoid(rx + bx_ref[...])
    log_a = -LRU_C * rg * _softplus(-lam_ref[...])
    a = jnp.exp(log_a)
    b = jnp.sqrt(1.0 - jnp.exp(2.0 * log_a)) * ig * xb
    d = 1
    while d < CHUNK:
        b = a * _shift_rows(b, d, 0.0) + b
        a = a * _shift_rows(a, d, 1.0)
        d *= 2
    h = b + a * h_ref[...]
    h_ref[...] = h[CHUNK - 1:CHUNK]
    o_ref[...] = (h * _gelu_tanh(g_ref[...])).astype(o_ref.dtype)


def lru_mixer(proj, conv_w, conv_b, w_a, b_a, w_x, b_x, lam):
    cw = jnp.zeros((8, GROUP_W), F32).at[0:4].set(conv_w)
    row = lambda v: v.reshape(1, GROUP_W)
    return _mixer_call(
        _lru_kernel, proj, [_pspec(P_CX), _pspec(P_CG)],
        [cw, row(conv_b), w_a.astype(BF16), row(b_a), w_x.astype(BF16), row(b_x), row(lam)],
        [(1, 8, GROUP_W), (1, GROUP_W)], "rglru_mixer")


ROUTE_TB = 256


def _topk_rows(s, k):
    n = s.shape[0]
    rows = lax.broadcasted_iota(jnp.int32, s.shape, 0)
    vals, ids = [], []
    for _ in range(k):
        m = jnp.max(s, axis=0, keepdims=True)
        idx = jnp.min(jnp.where(s == m, rows, n), axis=0, keepdims=True)
        vals.append(m)
        ids.append(idx)
        s = jnp.where(rows == idx, NEG_BIG, s)
    return jnp.concatenate(vals, axis=0), jnp.concatenate(ids, axis=0)


def _route_kernel(x_ref, g_ref, wq_ref, sk_ref, ex_ref, gt_ref):
    h2 = _rms(x_ref[...], g_ref[...]).astype(BF16)
    q = _dot(h2, wq_ref[...])
    for h in range(PEER_HEADS):
        tops = []
        for p in range(2):
            col = (h * 2 + p) * PEER_NKEYS
            sc = _dot_nt(sk_ref[h * 2 + p], q[:, col:col + PEER_NKEYS].astype(BF16))
            tops.append(_topk_rows(sc, PEER_TOPK))
        (s1, i1), (s2, i2) = tops
        sub = lax.broadcasted_iota(jnp.int32, (8, s1.shape[1]), 0)
        half = PEER_TOPK // 2
        cs = [s1[0:1] + s2[0:half], s1[0:1] + s2[half:PEER_TOPK]]
        ci = [i1[0:1] * PEER_NKEYS + i2[0:half], i1[0:1] * PEER_NKEYS + i2[half:PEER_TOPK]]
        for a in range(1, half):
            n_valid = PEER_TOPK // (a + 1)
            cs.append(jnp.where(sub < n_valid, s1[a:a + 1] + s2[0:half], NEG_BIG))
            ci.append(i1[a:a + 1] * PEER_NKEYS + i2[0:half])
        cs.append(s1[half:PEER_TOPK] + s2[0:1])
        ci.append(i1[half:PEER_TOPK] * PEER_NKEYS + i2[0:1])
        cand_s = jnp.concatenate(cs, axis=0)
        cand_i = jnp.concatenate(ci, axis=0)
        best_s, best_pos = _topk_rows(cand_s, PEER_TOPK)
        pos = lax.broadcasted_iota(jnp.int32, cand_s.shape, 0)
        experts = jnp.concatenate(
            [jnp.sum(jnp.where(pos == best_pos[r:r + 1], cand_i, 0), axis=0, keepdims=True)
             for r in range(PEER_TOPK)], axis=0)
        ew = jnp.exp(best_s - best_s[0:1])
        gates = ew / jnp.sum(ew, axis=0, keepdims=True)
        ex_ref[h * PEER_TOPK:(h + 1) * PEER_TOPK, :] = experts
        gt_ref[h * PEER_TOPK:(h + 1) * PEER_TOPK, :] = gates


def peer_route(x, gain, wq_bf, sk_bf):
    m, d = x.shape
    nq = wq_bf.shape[1]
    assert m % ROUTE_TB == 0, m
    return pl.pallas_call(
        _route_kernel,
        grid=(m // ROUTE_TB,),
        in_specs=[pl.BlockSpec((ROUTE_TB, d), lambda i: (i, 0)),
                  pl.BlockSpec((1, d), lambda i: (0, 0)),
                  pl.BlockSpec((d, nq), lambda i: (0, 0)),
                  pl.BlockSpec(sk_bf.shape, lambda i: (0, 0, 0))],
        out_specs=[pl.BlockSpec((PEER_SEL, ROUTE_TB), lambda i: (0, i)),
                   pl.BlockSpec((PEER_SEL, ROUTE_TB), lambda i: (0, i))],
        out_shape=[jax.ShapeDtypeStruct((PEER_SEL, m), jnp.int32),
                   jax.ShapeDtypeStruct((PEER_SEL, m), F32)],
        compiler_params=_params("parallel"),
        name="peer_route",
    )(x, gain.reshape(1, d), wq_bf, sk_bf)


EXP_TB = 64
GATE_W = 128
EXP_TG = 8
EXP_SLOTS = 4
EXP_AHEAD = 2
SUBLANES = 8
EXP_ROWS = EXP_TG * PEER_SEL
EXP_TILE_ROWS = EXP_ROWS // SUBLANES
TOK_TILE_ROWS = PEER_SEL // SUBLANES


def _expert_kernel(ids_ref, nxt_ref, x_ref, g_ref, og_ref, gt_ref, uv_ref, o_ref, *scratch, out_norm):
    bufs, (h_ref, sem) = scratch[:EXP_SLOTS], scratch[EXP_SLOTS:]
    d = x_ref.shape[1]
    n_word = d // 256
    n_sub = EXP_TB // EXP_TG
    step = pl.program_id(0)
    lane0 = (step % (GATE_W // EXP_TB)) * EXP_TB
    rows_per_iter = EXP_TILE_ROWS // (2 * EXP_TG)
    h_ref[...] = _rms(x_ref[...], g_ref[...])

    def issue_tile_row(src_ids, sub, slot, i):
        base = sub * EXP_ROWS + i * SUBLANES
        for k in range(SUBLANES):
            e = src_ids[base + k]
            pltpu.make_async_copy(uv_ref.at[e], bufs[slot].at[i, :, k, :], sem.at[slot]).start(priority=k % 2)

    def issue_all(src_ids, sub, slot):
        def body(i, carry):
            issue_tile_row(src_ids, sub, slot, i)
            return carry
        lax.fori_loop(0, EXP_TILE_ROWS, body, 0)

    def wait(slot):
        pltpu.make_async_copy(bufs[(slot + 1) % EXP_SLOTS], bufs[slot], sem.at[slot]).wait()

    lane = lax.broadcasted_iota(jnp.int32, (PEER_SEL, GATE_W), 1)

    def compute(sub, slot, ahead):
        def request(first_row):
            if ahead is not None:
                for ii in range(rows_per_iter):
                    issue_tile_row(ahead[0], ahead[1], ahead[2], first_row + ii)

        def u_side(t, act):
            n = sub * EXP_TG + t
            request(t * rows_per_iter)
            xn = h_ref[pl.ds(n, 1), :]
            parts = []
            for g in range(TOK_TILE_ROWS):
                acc = None
                for c in range(n_word):
                    lo, hi = _unpack_pair(bufs[slot][t * TOK_TILE_ROWS + g, c])
                    term = (lo * xn[:, c * 128:(c + 1) * 128]
                            + hi * xn[:, d // 2 + c * 128:d // 2 + (c + 1) * 128])
                    acc = term if acc is None else acc + term
                parts.append(acc)
            s_col = jnp.sum(jnp.concatenate(parts, axis=0), axis=-1, keepdims=True)
            return jnp.where(lane == lane0 + n, s_col, act)

        act = lax.fori_loop(0, EXP_TG, u_side, jnp.zeros((PEER_SEL, GATE_W), F32))
        coeff = gt_ref[...] * _gelu_tanh(act)

        def v_side(t, carry):
            n = sub * EXP_TG + t
            request((EXP_TG + t) * rows_per_iter)
            c_col = jnp.sum(jnp.where(lane == lane0 + n, coeff, 0.0), axis=-1, keepdims=True)
            c_b = jnp.broadcast_to(c_col, (PEER_SEL, 128))
            outs_lo, outs_hi = [], []
            for c in range(n_word):
                acc_lo = acc_hi = None
                for g in range(TOK_TILE_ROWS):
                    lo, hi = _unpack_pair(bufs[slot][t * TOK_TILE_ROWS + g, n_word + c])
                    cg = c_b[g * SUBLANES:(g + 1) * SUBLANES]
                    acc_lo = lo * cg if acc_lo is None else acc_lo + lo * cg
                    acc_hi = hi * cg if acc_hi is None else acc_hi + hi * cg
                outs_lo.append(jnp.sum(acc_lo, axis=0, keepdims=True))
                outs_hi.append(jnp.sum(acc_hi, axis=0, keepdims=True))
            row = x_ref[pl.ds(n, 1), :] + jnp.concatenate(outs_lo + outs_hi, axis=1)
            o_ref[pl.ds(n, 1), :] = _rms(row, og_ref[...]) if out_norm else row
            return carry

        lax.fori_loop(0, EXP_TG, v_side, 0)

    @pl.when(step == 0)
    def _():
        for s in range(EXP_AHEAD):
            issue_all(ids_ref, s, s)

    def group(q, carry):
        for r in range(EXP_SLOTS):
            sub = q * EXP_SLOTS + r
            wait(r)
            nxt_slot = (r + EXP_AHEAD) % EXP_SLOTS
            if r + EXP_AHEAD < EXP_SLOTS:
                compute(sub, r, (ids_ref, sub + EXP_AHEAD, nxt_slot))
            else:
                in_step = sub + EXP_AHEAD < n_sub
                pl.when(in_step)(functools.partial(compute, sub, r, (ids_ref, sub + EXP_AHEAD, nxt_slot)))
                pl.when(jnp.logical_not(in_step))(
                    functools.partial(compute, sub, r, (nxt_ref, r + EXP_AHEAD - EXP_SLOTS, nxt_slot)))
        return carry

    lax.fori_loop(0, n_sub // EXP_SLOTS, group, 0)

    @pl.when(step + 1 == pl.num_programs(0))
    def _():
        for s in range(EXP_AHEAD):
            wait(s)


def _pack_bf16_pairs(t):
    e, d = t.shape
    bits = lax.bitcast_convert_type(t.astype(BF16), jnp.uint16).astype(jnp.uint32)
    words = bits[:, :d // 2] | (bits[:, d // 2:] << 16)
    return words.reshape(e, d // 256, 128)


def _unpack_pair(w):
    lo = lax.bitcast_convert_type(w << 16, F32)
    hi = lax.bitcast_convert_type(w & jnp.uint32(0xFFFF0000), F32)
    return lo, hi


def peer_experts(x, gain, ids_flat, gates_t, u, v, out_gain=None):
    m, d = x.shape
    assert m % GATE_W == 0 and d % 256 == 0, (m, d)
    n_blk = m // EXP_TB
    uv_tiles = jnp.concatenate([_pack_bf16_pairs(u), _pack_bf16_pairs(v)], axis=1)
    ids_spec = lambda index_map: pl.BlockSpec((EXP_TB * PEER_SEL,), index_map, memory_space=pltpu.SMEM)
    return pl.pallas_call(
        functools.partial(_expert_kernel, out_norm=out_gain is not None),
        grid=(n_blk,),
        in_specs=[ids_spec(lambda i: (i,)),
                  ids_spec(lambda i: (jnp.minimum(i + 1, n_blk - 1),)),
                  pl.BlockSpec((EXP_TB, d), lambda i: (i, 0)),
                  pl.BlockSpec((1, d), lambda i: (0, 0)),
                  pl.BlockSpec((1, d), lambda i: (0, 0)),
                  pl.BlockSpec((PEER_SEL, GATE_W), lambda i: (0, i // (GATE_W // EXP_TB))),
                  pl.BlockSpec(memory_space=pl.ANY)],
        out_specs=pl.BlockSpec((EXP_TB, d), lambda i: (i, 0)),
        out_shape=jax.ShapeDtypeStruct((m, d), F32),
        scratch_shapes=[pltpu.VMEM((EXP_TILE_ROWS, 2 * d // 256, SUBLANES, 128), jnp.uint32)] * EXP_SLOTS + [
                        pltpu.VMEM((EXP_TB, d), F32),
                        pltpu.SemaphoreType.DMA((EXP_SLOTS,))],
        compiler_params=_params("arbitrary"),
        name="peer_experts",
    )(ids_flat, ids_flat, x, gain.reshape(1, d), (gain if out_gain is None else out_gain).reshape(1, d), gates_t, uv_tiles)


def _relayout_w_in(w):
    d = w.shape[0]
    z = lambda n: jnp.zeros((d, n), w.dtype)
    return jnp.concatenate([
        w[:, 0:2048],
        w[:, 2056:4104],
        w[:, 4104:5128],
        w[:, 5640:6664],
        w[:, 5128:5640],
        w[:, 2048:2056], z(120),
        w[:, 6664:6680], z(112),
        z(256)], axis=1)


def kernel(x, norm_mix, norm_ffn, norm_final, w_in, w_out, gdn_conv_w, gdn_a_log, gdn_dt_bias, gdn_norm,
           hgrn_lb_logits, hgrn_norm, lru_conv_w, lru_conv_b, lru_w_a, lru_b_a, lru_w_x, lru_b_x, lru_lambda,
           gla_w_gate, gla_b_gate, gla_norm, peer_w_q, peer_sub_keys, peer_u, peer_v):
    bsz, t, d = x.shape
    m = bsz * t
    depth = w_in.shape[0]
    cc = _chunk_consts()
    lb_all = jnp.cumsum(jax.nn.softmax(hgrn_lb_logits.astype(F32), axis=0), axis=0)
    lb_all = lb_all - lb_all[0]
    tm = 512 if m % 512 == 0 else 256
    xf = x.reshape(m, d)
    for l in range(depth):
        w_in_l = _relayout_w_in(w_in[l]).astype(BF16)
        tm_in = 2 * tm if m % (2 * tm) == 0 else tm
        proj = norm_matmul(xf, norm_mix[l], w_in_l, tm_in, P_WIDTH // 4).reshape(bsz, t, P_WIDTH)
        y_a = gdn_mixer(proj, gdn_conv_w[l], gdn_a_log[l], gdn_dt_bias[l], gdn_norm[l], cc)
        y_b = hgrn_mixer(proj, lb_all[l], hgrn_norm[l], cc)
        y_c = lru_mixer(proj, lru_conv_w[l], lru_conv_b[l], lru_w_a[l], lru_b_a[l], lru_w_x[l], lru_b_x[l],
                        lru_lambda[l])
        y_d = gla_mixer(proj, gla_w_gate[l], gla_b_gate[l], gla_norm[l], cc)
        ys = [y.reshape(m, GROUP_W) for y in (y_a, y_b, y_c, y_d)]
        xf = out_proj(ys, w_out[l].reshape(4, GROUP_W, d).astype(BF16), xf, tm)
        sk = peer_sub_keys[l].reshape(PEER_HEADS * 2, PEER_NKEYS, -1).astype(BF16)
        experts_t, gates_t = peer_route(xf, norm_ffn[l], peer_w_q[l].astype(BF16), sk)
        ids_flat = experts_t.T.reshape(m * PEER_SEL)
        xf = peer_experts(xf, norm_ffn[l], ids_flat, gates_t, peer_u[l], peer_v[l],
                          out_gain=norm_final if l == depth - 1 else None)
    return xf.reshape(bsz, t, d)
```

```python
import functools
import math

import numpy as np
import jax
import jax.numpy as jnp
from jax import lax
from jax.experimental import pallas as pl
from jax.experimental.pallas import tpu as pltpu

F32 = jnp.float32
BF16 = jnp.bfloat16
EPS = 1e-6
CHUNK = 64
GROUP_W = 512
N_HEADS = 4
HEAD_DV = 128
GLA_DK = 64
GLA_RANK = 16
GLA_TAU = 16.0
LRU_C = 8.0
LRU_BLOCKS = 4
PEER_HEADS = 8
PEER_NKEYS = 128
PEER_TOPK = 16
PEER_SEL = PEER_HEADS * PEER_TOPK
NEG_BIG = -1e30
VMEM_LIMIT = 56 * 1024 * 1024

P_AQ, P_AK, P_AV, P_AZ, P_BQ, P_BF, P_BI, P_BG, P_CX, P_CG, P_DV, P_DG, P_DQK, P_SM = range(14)
P_BLOCKS = 14
P_WIDTH = P_BLOCKS * GROUP_W


def _dot(a, b):
    return jnp.dot(a, b, preferred_element_type=F32)


def _dot_nt(a, b):
    return lax.dot_general(a, b, (((1,), (1,)), ((), ())), preferred_element_type=F32)


def _split3(a):
    a1 = a.astype(BF16)
    r1 = a - a1.astype(F32)
    a2 = r1.astype(BF16)
    r2 = r1 - a2.astype(F32)
    return a1, a2, r2.astype(BF16)


def _mm3(a, b, nt=False):
    d = _dot_nt if nt else _dot
    a1, a2, _ = _split3(a)
    b1, b2, _ = _split3(b)
    return d(a1, b1) + (d(a1, b2) + d(a2, b1))


def _mm_exact_l(w_bf, x):
    x1, x2, x3 = _split3(x)
    return _dot(w_bf, x1) + (_dot(w_bf, x2) + _dot(w_bf, x3))


def _mm_exact_r(x, w_bf):
    x1, x2, x3 = _split3(x)
    return _dot(x1, w_bf) + (_dot(x2, w_bf) + _dot(x3, w_bf))


def _sigmoid(x):
    return 1.0 / (1.0 + jnp.exp(-x))


def _silu(x):
    return x * _sigmoid(x)


def _softplus(x):
    return jnp.maximum(x, 0.0) + jnp.log(1.0 + jnp.exp(-jnp.abs(x)))


def _gelu_tanh(x):
    return 0.5 * x * (1.0 + jnp.tanh(math.sqrt(2.0 / math.pi) * (x + 0.044715 * (x * x * x))))


def _rms(x, gain):
    return x * lax.rsqrt(jnp.mean(x * x, axis=-1, keepdims=True) + EPS) * gain


def _params(*sem):
    return pltpu.CompilerParams(dimension_semantics=sem, vmem_limit_bytes=VMEM_LIMIT)


def _norm_matmul_kernel(x_ref, g_ref, w_ref, o_ref, h_ref):
    @pl.when(pl.program_id(1) == 0)
    def _():
        h_ref[...] = _rms(x_ref[...], g_ref[...]).astype(BF16)

    o_ref[...] = _dot(h_ref[...], w_ref[...]).astype(o_ref.dtype)


def norm_matmul(x, gain, w_bf, tm, tn, out_dtype=F32):
    m, d = x.shape
    n = w_bf.shape[1]
    return pl.pallas_call(
        _norm_matmul_kernel,
        grid=(m // tm, n // tn),
        in_specs=[pl.BlockSpec((tm, d), lambda i, j: (i, 0)),
                  pl.BlockSpec((1, d), lambda i, j: (0, 0)),
                  pl.BlockSpec((d, tn), lambda i, j: (0, j))],
        out_specs=pl.BlockSpec((tm, tn), lambda i, j: (i, j)),
        out_shape=jax.ShapeDtypeStruct((m, n), out_dtype),
        scratch_shapes=[pltpu.VMEM((tm, d), BF16)],
        compiler_params=_params("parallel", "arbitrary"),
        name="norm_matmul",
    )(x, gain.reshape(1, d), w_bf)


def _out_proj_kernel(ya_ref, yb_ref, yc_ref, yd_ref, w_ref, x_ref, o_ref):
    acc = x_ref[...]
    for g, y_ref in enumerate((ya_ref, yb_ref, yc_ref, yd_ref)):
        acc = acc + _dot(y_ref[...], w_ref[g])
    o_ref[...] = acc


def out_proj(ys, w_bf, x, tm):
    m, d = x.shape
    y_spec = pl.BlockSpec((tm, GROUP_W), lambda i: (i, 0))
    return pl.pallas_call(
        _out_proj_kernel,
        grid=(m // tm,),
        in_specs=[y_spec, y_spec, y_spec, y_spec,
                  pl.BlockSpec((4, GROUP_W, d), lambda i: (0, 0, 0)),
                  pl.BlockSpec((tm, d), lambda i: (i, 0))],
        out_specs=pl.BlockSpec((tm, d), lambda i: (i, 0)),
        out_shape=jax.ShapeDtypeStruct((m, d), F32),
        compiler_params=_params("parallel"),
        name="out_proj",
    )(*ys, w_bf, x)


def _final_norm_kernel(x_ref, g_ref, o_ref):
    o_ref[...] = _rms(x_ref[...], g_ref[...])


def final_norm(x, gain, tm):
    m, d = x.shape
    return pl.pallas_call(
        _final_norm_kernel,
        grid=(m // tm,),
        in_specs=[pl.BlockSpec((tm, d), lambda i: (i, 0)), pl.BlockSpec((1, d), lambda i: (0, 0))],
        out_specs=pl.BlockSpec((tm, d), lambda i: (i, 0)),
        out_shape=jax.ShapeDtypeStruct((m, d), F32),
        compiler_params=_params("parallel"),
        name="final_norm",
    )(x, gain.reshape(1, d))


def _chunk_consts():
    i = np.arange(CHUNK)[:, None]
    t = np.arange(CHUNK)[None, :]
    tril = (t <= i).astype(np.float32)
    after = (t > i).astype(np.float32)
    levels, low, masks = [], [], []
    s = CHUNK // 2
    while s >= 1:
        pos = i % (2 * s)
        ref_row = (i // (2 * s)) * (2 * s) + s
        lower = pos >= s
        w = np.where(lower, (t > ref_row) & (t <= i), (t > i) & (t <= ref_row)).astype(np.float32)
        levels.append(w)
        low.append(np.broadcast_to(lower, (CHUNK, 128)).astype(np.float32))
        same = (i // (2 * s)) == (t // (2 * s))
        masks.append((same & lower & ((t % (2 * s)) < s)).astype(np.float32))
        s //= 2
    wall = np.concatenate([tril, after] + levels, axis=0)
    return (jnp.asarray(wall, BF16), jnp.asarray(np.stack(low), F32), jnp.asarray(np.stack(masks), F32),
            jnp.asarray(tril, BF16), jnp.asarray(tril.T, BF16))


N_LEVELS = int(math.log2(CHUNK))


def _gla_chunks(qs_all, k_all, v_all, lg_all, sts, dk, wall, low_ref, mask_ref):
    nh = len(sts)
    hs = lambda a, h, w: a[:, h * w:(h + 1) * w]
    e = _mm_exact_l(wall, lg_all)
    bcum = e[0:CHUNK]
    brev = e[CHUNK:2 * CHUNK]
    q_bf = qs_all.astype(BF16)
    k_bf = k_all.astype(BF16)
    eye = _eye()
    attn = [jnp.where(eye, _dot_nt(hs(q_bf, h, dk), hs(k_bf, h, dk)), 0.0) for h in range(nh)]
    for lv in range(N_LEVELS):
        f = jnp.exp(e[(2 + lv) * CHUNK:(3 + lv) * CHUNK])
        z = (jnp.where(low_ref[lv][:, 0:1] > 0.5, qs_all, k_all) * f).astype(BF16)
        keep = mask_ref[lv] > 0.5
        attn = [attn[h] + jnp.where(keep, _dot_nt(hs(z, h, dk), hs(z, h, dk)), 0.0) for h in range(nh)]
    qd = (qs_all * jnp.exp(bcum)).astype(BF16)
    kdec = (k_all * jnp.exp(brev)).astype(BF16)
    eb = jnp.exp(bcum[CHUNK - 1:CHUNK])
    v_bf = v_all.astype(BF16)
    outs = [_dot_nt(hs(qd, h, dk), sts[h].astype(BF16)) + _dot(attn[h].astype(BF16), hs(v_bf, h, HEAD_DV))
            for h in range(nh)]
    new = [sts[h] * hs(eb, h, dk) + _dot(hs(v_all, h, HEAD_DV).T.astype(BF16), hs(kdec, h, dk))
           for h in range(nh)]
    return outs, new


def _eye():
    r = lax.broadcasted_iota(jnp.int32, (CHUNK, CHUNK), 0)
    c = lax.broadcasted_iota(jnp.int32, (CHUNK, CHUNK), 1)
    return r == c


def _gated_out(o, gate, nw):
    o = o * lax.rsqrt(jnp.mean(o * o, axis=-1, keepdims=True) + EPS) * nw
    return o * _silu(gate)


def _hgrn_kernel(q_ref, f_ref, i_ref, g_ref, lb_ref, nw_ref, wall_ref, low_ref, mask_ref, o_ref, st_ref):
    @pl.when(pl.program_id(1) == 0)
    def _():
        st_ref[...] = jnp.zeros_like(st_ref)

    rows = lambda fn: jnp.concatenate([fn(bb) for bb in range(MIX_BB)], axis=1)
    lb = rows(lambda bb: lb_ref[...])
    sg = _sigmoid(rows(lambda bb: f_ref[bb]))
    lg_all = jnp.log(lb + (1.0 - lb) * sg)
    k_all = (1.0 - lb) * (1.0 - sg)
    q_all = _silu(rows(lambda bb: q_ref[bb])) * (HEAD_DV ** -0.5)
    v_all = rows(lambda bb: i_ref[bb])
    _gla_finish(q_all, k_all, v_all, lg_all, HEAD_DV, g_ref, nw_ref, wall_ref, low_ref, mask_ref, o_ref, st_ref)


def _gla_finish(q_all, k_all, v_all, lg_all, dk, g_ref, nw_ref, wall_ref, low_ref, mask_ref, o_ref, st_ref):
    pairs = [(bb, h) for bb in range(MIX_BB) for h in range(N_HEADS)]
    sts = [st_ref[bb, h] for bb, h in pairs]
    outs, new = _gla_chunks(q_all, k_all, v_all, lg_all, sts, dk, wall_ref[...], low_ref, mask_ref)
    for (bb, h), o, st_new in zip(pairs, outs, new):
        sl = slice(h * HEAD_DV, (h + 1) * HEAD_DV)
        st_ref[bb, h] = st_new
        o_ref[bb, :, sl] = _gated_out(o, g_ref[bb, :, sl], nw_ref[...]).astype(o_ref.dtype)


def _gla_kernel(qk_ref, v_ref, g_ref, sm_ref, wg_ref, bg_ref, nw_ref, wall_ref, low_ref, mask_ref, o_ref, st_ref):
    @pl.when(pl.program_id(1) == 0)
    def _():
        st_ref[...] = jnp.zeros_like(st_ref)

    rows = lambda fn: jnp.concatenate([fn(bb) for bb in range(MIX_BB)], axis=1)
    nq = N_HEADS * GLA_DK
    gate_in = rows(lambda bb: _mm3(sm_ref[bb], wg_ref[...]) + bg_ref[...])
    lg_all = -_softplus(-gate_in) * (1.0 / GLA_TAU)
    q_all = rows(lambda bb: qk_ref[bb, :, 0:nq]) * (GLA_DK ** -0.5)
    k_all = rows(lambda bb: qk_ref[bb, :, nq:2 * nq])
    v_all = rows(lambda bb: v_ref[bb])
    _gla_finish(q_all, k_all, v_all, lg_all, GLA_DK, g_ref, nw_ref, wall_ref, low_ref, mask_ref, o_ref, st_ref)


MIX_BB = 4


def _pspec(col_block, width=GROUP_W):
    return pl.BlockSpec((MIX_BB, CHUNK, width), lambda b, c: (b, c, col_block))


def _full(shape):
    nd = len(shape)
    return pl.BlockSpec(shape, lambda b, c: (0,) * nd)


def _per_batch_row(body, n_blocked, n_consts):
    def kern(*refs):
        blocked = refs[:n_blocked]
        consts = refs[n_blocked:n_blocked + n_consts]
        out = refs[n_blocked + n_consts]
        scratch = refs[n_blocked + n_consts + 1:]
        for bb in range(MIX_BB):
            body(*[r.at[bb] for r in blocked], *consts, out.at[bb], *[r.at[bb] for r in scratch])
    return kern


def _mixer_call(body, proj, col_specs, consts, scratch, name, per_row=True):
    bsz, t, _ = proj.shape
    assert bsz % MIX_BB == 0 and t % CHUNK == 0, (bsz, t)
    return pl.pallas_call(
        _per_batch_row(body, len(col_specs), len(consts)) if per_row else body,
        grid=(bsz // MIX_BB, t // CHUNK),
        in_specs=col_specs + [_full(c.shape) for c in consts],
        out_specs=pl.BlockSpec((MIX_BB, CHUNK, GROUP_W), lambda b, c: (b, c, 0)),
        out_shape=jax.ShapeDtypeStruct((bsz, t, GROUP_W), BF16),
        scratch_shapes=[pltpu.VMEM((MIX_BB,) + shape, F32) for shape in scratch],
        compiler_params=_params("parallel", "arbitrary"),
        name=name,
    )(*([proj] * len(col_specs)), *consts)


def hgrn_mixer(proj, lb, nw, cc):
    wall, low, masks, _, _ = cc
    return _mixer_call(
        _hgrn_kernel, proj, [_pspec(P_BQ), _pspec(P_BF), _pspec(P_BI), _pspec(P_BG)],
        [lb.reshape(1, GROUP_W), nw.reshape(1, HEAD_DV), wall, low, masks],
        [(N_HEADS, HEAD_DV, HEAD_DV)], "hgrn2_mixer", per_row=False)


def gla_mixer(proj, w_gate, b_gate, nw, cc):
    wall, low, masks, _, _ = cc
    wg = jnp.zeros((128, N_HEADS * GLA_DK), F32).at[0:GLA_RANK].set(w_gate)
    return _mixer_call(
        _gla_kernel, proj, [_pspec(P_DQK), _pspec(P_DV), _pspec(P_DG), _pspec(P_SM * 4 + 1, 128)],
        [wg, b_gate.reshape(1, N_HEADS * GLA_DK), nw.reshape(1, HEAD_DV), wall, low, masks],
        [(N_HEADS, HEAD_DV, GLA_DK)], "gla_mixer", per_row=False)


def _causal_conv(x_ref, tail_ref, slot, cw):
    cur = x_ref[...]
    ext = jnp.concatenate([tail_ref[slot], cur], axis=0)
    y = ext[5:5 + CHUNK] * cw[0:1]
    for w in range(1, 4):
        y = y + ext[5 + w:5 + w + CHUNK] * cw[w:w + 1]
    tail_ref[slot] = cur[CHUNK - 8:CHUNK]
    return y


def _l2n(t):
    return t * lax.rsqrt(jnp.sum(t * t, axis=-1, keepdims=True) + EPS)


def _gdn_kernel(q_ref, k_ref, v_ref, z_ref, sm_ref, cw_ref, alog_ref, dtb_ref, nw_ref, tril_ref, trilt_ref,
                o_ref, tail_ref, s_ref):
    @pl.when(pl.program_id(1) == 0)
    def _():
        tail_ref[...] = jnp.zeros_like(tail_ref)
        s_ref[...] = jnp.zeros_like(s_ref)

    r = lax.broadcasted_iota(jnp.int32, (CHUNK, CHUNK), 0)
    c = lax.broadcasted_iota(jnp.int32, (CHUNK, CHUNK), 1)
    causal = r >= c
    strict = r > c
    eye = (r == c).astype(F32)
    scale = HEAD_DV ** -0.5
    pairs = [(bb, h) for bb in range(MIX_BB) for h in range(N_HEADS)]
    each = lambda fn: [fn(i) for i in range(len(pairs))]
    q_rows, k_rows, v_rows, gcum, gcum_t, beta_rows = [], [], [], [], [], []
    for bb in range(MIX_BB):
        tails = tail_ref.at[bb]
        q_rows.append(_silu(_causal_conv(q_ref.at[bb], tails, 0, cw_ref[0])))
        k_rows.append(_silu(_causal_conv(k_ref.at[bb], tails, 1, cw_ref[1])))
        v_rows.append(_silu(_causal_conv(v_ref.at[bb], tails, 2, cw_ref[2])))
        sm = sm_ref[bb]
        g_full = -(jnp.exp(alog_ref[...]) * _softplus(sm + dtb_ref[...]))
        beta_rows.append(_sigmoid(sm))
        gcum.append(_mm_exact_l(tril_ref[...], g_full))
        gcum_t.append(_mm_exact_r(g_full.T, trilt_ref[...]))
    head = lambda rows, i: rows[pairs[i][0]][:, pairs[i][1] * HEAD_DV:(pairs[i][1] + 1) * HEAD_DV]
    qh = each(lambda i: _l2n(head(q_rows, i)) * scale)
    kh = each(lambda i: _l2n(head(k_rows, i)))
    vh = each(lambda i: head(v_rows, i))
    gc = each(lambda i: gcum[pairs[i][0]][:, pairs[i][1]:pairs[i][1] + 1])
    gl = each(lambda i: gc[i][CHUNK - 1:CHUNK])
    beta = each(lambda i: beta_rows[pairs[i][0]][:, N_HEADS + pairs[i][1]:N_HEADS + pairs[i][1] + 1])
    decay = each(lambda i: jnp.exp(jnp.where(
        causal, gc[i] - gcum_t[pairs[i][0]][pairs[i][1]:pairs[i][1] + 1, :], NEG_BIG)))
    kb = each(lambda i: kh[i] * beta[i])
    x = each(lambda i: -jnp.where(strict, _mm3(kb[i], kh[i], nt=True) * decay[i], 0.0))
    tinv = each(lambda i: eye + x[i])
    p = x
    for _ in range(N_LEVELS - 1):
        p = each(lambda i: _mm3(p[i], p[i]))
        tinv = each(lambda i: tinv[i] + _mm3(tinv[i], p[i]))
    egc = each(lambda i: jnp.exp(gc[i]))
    u = each(lambda i: _mm3(tinv[i], vh[i] * beta[i]))
    w = each(lambda i: _mm3(tinv[i], kb[i] * egc[i]))
    s = each(lambda i: s_ref[pairs[i][0], pairs[i][1]])
    v_new = each(lambda i: u[i] - _mm3(w[i], s[i]))
    attn = each(lambda i: jnp.where(causal, _mm3(qh[i], kh[i], nt=True) * decay[i], 0.0))
    o = each(lambda i: _mm3(qh[i] * egc[i], s[i]) + _mm3(attn[i], v_new[i]))
    s_new = each(lambda i: s[i] * jnp.exp(gl[i]) + _mm3((kh[i] * jnp.exp(gl[i] - gc[i])).T, v_new[i]))
    for i, (bb, h) in enumerate(pairs):
        sl = slice(h * HEAD_DV, (h + 1) * HEAD_DV)
        s_ref[bb, h] = s_new[i]
        o_ref[bb, :, sl] = _gated_out(o[i], z_ref[bb, :, sl], nw_ref[...]).astype(o_ref.dtype)


def gdn_mixer(proj, conv_w, a_log, dt_bias, nw, cc):
    _, _, _, tril, trilt = cc
    cw = jnp.zeros((3, 8, GROUP_W), F32).at[:, 0:4].set(conv_w.reshape(4, 3, GROUP_W).transpose(1, 0, 2))
    alog = jnp.zeros((1, 128), F32).at[0, 0:N_HEADS].set(a_log)
    dtb = jnp.zeros((1, 128), F32).at[0, 0:N_HEADS].set(dt_bias)
    return _mixer_call(
        _gdn_kernel, proj,
        [_pspec(P_AQ), _pspec(P_AK), _pspec(P_AV), _pspec(P_AZ), _pspec(P_SM * 4, 128)],
        [cw, alog, dtb, nw.reshape(1, HEAD_DV), tril, trilt],
        [(3, 8, GROUP_W), (N_HEADS, HEAD_DV, HEAD_DV)], "gdn_mixer", per_row=False)


def _shift_rows(x, d, fill):
    rows = lax.broadcasted_iota(jnp.int32, x.shape, 0)
    return jnp.where(rows >= d, pltpu.roll(x, d, 0), fill)


def _lru_kernel(x_ref, g_ref, cw_ref, cb_ref, wa_ref, ba_ref, wx_ref, bx_ref, lam_ref, o_ref, tail_ref, h_ref):
    @pl.when(pl.program_id(1) == 0)
    def _():
        tail_ref[...] = jnp.zeros_like(tail_ref)
        h_ref[...] = jnp.zeros_like(h_ref)

    xb = _causal_conv(x_ref, tail_ref, 0, cw_ref[...]) + cb_ref[...]
    xb_bf = xb.astype(BF16)
    bw = GROUP_W // LRU_BLOCKS
    ra = jnp.concatenate([_dot(xb_bf[:, g * bw:(g + 1) * bw], wa_ref[g]) for g in range(LRU_BLOCKS)], axis=1)
    rx = jnp.concatenate([_dot(xb_bf[:, g * bw:(g + 1) * bw], wx_ref[g]) for g in range(LRU_BLOCKS)], axis=1)
    rg = _sigmoid(ra + ba_ref[...])
    ig = _sigmoid(rx + bx_ref[...])
    log_a = -LRU_C * rg * _softplus(-lam_ref[...])
    a = jnp.exp(log_a)
    b = jnp.sqrt(1.0 - jnp.exp(2.0 * log_a)) * ig * xb
    d = 1
    while d < CHUNK:
        b = a * _shift_rows(b, d, 0.0) + b
        a = a * _shift_rows(a, d, 1.0)
        d *= 2
    h = b + a * h_ref[...]
    h_ref[...] = h[CHUNK - 1:CHUNK]
    o_ref[...] = (h * _gelu_tanh(g_ref[...])).astype(o_ref.dtype)


def lru_mixer(proj, conv_w, conv_b, w_a, b_a, w_x, b_x, lam):
    cw = jnp.zeros((8, GROUP_W), F32).at[0:4].set(conv_w)
    row = lambda v: v.reshape(1, GROUP_W)
    return _mixer_call(
        _lru_kernel, proj, [_pspec(P_CX), _pspec(P_CG)],
        [cw, row(conv_b), w_a.astype(BF16), row(b_a), w_x.astype(BF16), row(b_x), row(lam)],
        [(1, 8, GROUP_W), (1, GROUP_W)], "rglru_mixer")


ROUTE_TB = 256


def _topk_rows(s, k):
    n = s.shape[0]
    rows = lax.broadcasted_iota(jnp.int32, s.shape, 0)
    vals, ids = [], []
    for _ in range(k):
        m = jnp.max(s, axis=0, keepdims=True)
        idx = jnp.min(jnp.where(s == m, rows, n), axis=0, keepdims=True)
        vals.append(m)
        ids.append(idx)
        s = jnp.where(rows == idx, NEG_BIG, s)
    return jnp.concatenate(vals, axis=0), jnp.concatenate(ids, axis=0)


def _route_kernel(x_ref, g_ref, wq_ref, sk_ref, ex_ref, gt_ref):
    h2 = _rms(x_ref[...], g_ref[...]).astype(BF16)
    q = _dot(h2, wq_ref[...])
    for h in range(PEER_HEADS):
        tops = []
        for p in range(2):
            col = (h * 2 + p) * PEER_NKEYS
            sc = _dot_nt(sk_ref[h * 2 + p], q[:, col:col + PEER_NKEYS].astype(BF16))
            tops.append(_topk_rows(sc, PEER_TOPK))
        (s1, i1), (s2, i2) = tops
        sub = lax.broadcasted_iota(jnp.int32, (8, s1.shape[1]), 0)
        half = PEER_TOPK // 2
        cs = [s1[0:1] + s2[0:half], s1[0:1] + s2[half:PEER_TOPK]]
        ci = [i1[0:1] * PEER_NKEYS + i2[0:half], i1[0:1] * PEER_NKEYS + i2[half:PEER_TOPK]]
        for a in range(1, half):
            n_valid = PEER_TOPK // (a + 1)
            cs.append(jnp.where(sub < n_valid, s1[a:a + 1] + s2[0:half], NEG_BIG))
            ci.append(i1[a:a + 1] * PEER_NKEYS + i2[0:half])
        cs.append(s1[half:PEER_TOPK] + s2[0:1])
        ci.append(i1[half:PEER_TOPK] * PEER_NKEYS + i2[0:1])
        cand_s = jnp.concatenate(cs, axis=0)
        cand_i = jnp.concatenate(ci, axis=0)
        best_s, best_pos = _topk_rows(cand_s, PEER_TOPK)
        pos = lax.broadcasted_iota(jnp.int32, cand_s.shape, 0)
        experts = jnp.concatenate(
            [jnp.sum(jnp.where(pos == best_pos[r:r + 1], cand_i, 0), axis=0, keepdims=True)
             for r in range(PEER_TOPK)], axis=0)
        ew = jnp.exp(best_s - best_s[0:1])
        gates = ew / jnp.sum(ew, axis=0, keepdims=True)
        ex_ref[h * PEER_TOPK:(h + 1) * PEER_TOPK, :] = experts
        gt_ref[h * PEER_TOPK:(h + 1) * PEER_TOPK, :] = gates


def peer_route(x, gain, wq_bf, sk_bf):
    m, d = x.shape
    nq = wq_bf.shape[1]
    assert m % ROUTE_TB == 0, m
    return pl.pallas_call(
        _route_kernel,
        grid=(m // ROUTE_TB,),
        in_specs=[pl.BlockSpec((ROUTE_TB, d), lambda i: (i, 0)),
                  pl.BlockSpec((1, d), lambda i: (0, 0)),
                  pl.BlockSpec((d, nq), lambda i: (0, 0)),
                  pl.BlockSpec(sk_bf.shape, lambda i: (0, 0, 0))],
        out_specs=[pl.BlockSpec((PEER_SEL, ROUTE_TB), lambda i: (0, i)),
                   pl.BlockSpec((PEER_SEL, ROUTE_TB), lambda i: (0, i))],
        out_shape=[jax.ShapeDtypeStruct((PEER_SEL, m), jnp.int32),
                   jax.ShapeDtypeStruct((PEER_SEL, m), F32)],
        compiler_params=_params("parallel"),
        name="peer_route",
    )(x, gain.reshape(1, d), wq_bf, sk_bf)


EXP_TB = 64
GATE_W = 128
EXP_TG = 8
EXP_SLOTS = 4
EXP_AHEAD = 2
SUBLANES = 8
EXP_ROWS = EXP_TG * PEER_SEL
EXP_TILE_ROWS = EXP_ROWS // SUBLANES
TOK_TILE_ROWS = PEER_SEL // SUBLANES


def _expert_kernel(ids_ref, nxt_ref, x_ref, g_ref, gt_ref, uv_ref, o_ref, *scratch):
    bufs, (h_ref, sem) = scratch[:EXP_SLOTS], scratch[EXP_SLOTS:]
    d = x_ref.shape[1]
    n_word = d // 256
    n_sub = EXP_TB // EXP_TG
    step = pl.program_id(0)
    lane0 = (step % (GATE_W // EXP_TB)) * EXP_TB
    rows_per_iter = EXP_TILE_ROWS // (2 * EXP_TG)
    h_ref[...] = _rms(x_ref[...], g_ref[...])

    def issue_tile_row(src_ids, sub, slot, i):
        base = sub * EXP_ROWS + i * SUBLANES
        for k in range(SUBLANES):
            e = src_ids[base + k]
            pltpu.make_async_copy(uv_ref.at[e], bufs[slot].at[i, :, k, :], sem.at[slot]).start(priority=k % 2)

    def issue_all(src_ids, sub, slot):
        def body(i, carry):
            issue_tile_row(src_ids, sub, slot, i)
            return carry
        lax.fori_loop(0, EXP_TILE_ROWS, body, 0)

    def wait(slot):
        pltpu.make_async_copy(bufs[(slot + 1) % EXP_SLOTS], bufs[slot], sem.at[slot]).wait()

    lane = lax.broadcasted_iota(jnp.int32, (PEER_SEL, GATE_W), 1)

    def compute(sub, slot, ahead):
        def request(first_row):
            if ahead is not None:
                for ii in range(rows_per_iter):
                    issue_tile_row(ahead[0], ahead[1], ahead[2], first_row + ii)

        def u_side(t, act):
            n = sub * EXP_TG + t
            request(t * rows_per_iter)
            xn = h_ref[pl.ds(n, 1), :]
            parts = []
            for g in range(TOK_TILE_ROWS):
                acc = None
                for c in range(n_word):
                    lo, hi = _unpack_pair(bufs[slot][t * TOK_TILE_ROWS + g, c])
                    term = (lo * xn[:, c * 128:(c + 1) * 128]
                            + hi * xn[:, d // 2 + c * 128:d // 2 + (c + 1) * 128])
                    acc = term if acc is None else acc + term
                parts.append(acc)
            s_col = jnp.sum(jnp.concatenate(parts, axis=0), axis=-1, keepdims=True)
            return jnp.where(lane == lane0 + n, s_col, act)

        act = lax.fori_loop(0, EXP_TG, u_side, jnp.zeros((PEER_SEL, GATE_W), F32))
        coeff = gt_ref[...] * _gelu_tanh(act)

        def v_side(t, carry):
            n = sub * EXP_TG + t
            request((EXP_TG + t) * rows_per_iter)
            c_col = jnp.sum(jnp.where(lane == lane0 + n, coeff, 0.0), axis=-1, keepdims=True)
            c_b = jnp.broadcast_to(c_col, (PEER_SEL, 128))
            outs_lo, outs_hi = [], []
            for c in range(n_word):
                acc_lo = acc_hi = None
                for g in range(TOK_TILE_ROWS):
                    lo, hi = _unpack_pair(bufs[slot][t * TOK_TILE_ROWS + g, n_word + c])
                    cg = c_b[g * SUBLANES:(g + 1) * SUBLANES]
                    acc_lo = lo * cg if acc_lo is None else acc_lo + lo * cg
                    acc_hi = hi * cg if acc_hi is None else acc_hi + hi * cg
                outs_lo.append(jnp.sum(acc_lo, axis=0, keepdims=True))
                outs_hi.append(jnp.sum(acc_hi, axis=0, keepdims=True))
            o_ref[pl.ds(n, 1), :] = x_ref[pl.ds(n, 1), :] + jnp.concatenate(outs_lo + outs_hi, axis=1)
            return carry

        lax.fori_loop(0, EXP_TG, v_side, 0)

    @pl.when(step == 0)
    def _():
        for s in range(EXP_AHEAD):
            issue_all(ids_ref, s, s)

    def group(q, carry):
        for r in range(EXP_SLOTS):
            sub = q * EXP_SLOTS + r
            wait(r)
            nxt_slot = (r + EXP_AHEAD) % EXP_SLOTS
            if r + EXP_AHEAD < EXP_SLOTS:
                compute(sub, r, (ids_ref, sub + EXP_AHEAD, nxt_slot))
            else:
                in_step = sub + EXP_AHEAD < n_sub
                pl.when(in_step)(functools.partial(compute, sub, r, (ids_ref, sub + EXP_AHEAD, nxt_slot)))
                pl.when(jnp.logical_not(in_step))(
                    functools.partial(compute, sub, r, (nxt_ref, r + EXP_AHEAD - EXP_SLOTS, nxt_slot)))
        return carry

    lax.fori_loop(0, n_sub // EXP_SLOTS, group, 0)

    @pl.when(step + 1 == pl.num_programs(0))
    def _():
        for s in range(EXP_AHEAD):
            wait(s)


def _pack_expert_tables(u, v):
    e, d = u.shape
    rows = jnp.stack([u, v], axis=1)
    bits = lax.bitcast_convert_type(rows.astype(BF16), jnp.uint16).astype(jnp.uint32)
    words = bits[:, :, :d // 2] | (bits[:, :, d // 2:] << 16)
    return words.reshape(e, 2 * (d // 256), 128)


def _unpack_pair(w):
    lo = lax.bitcast_convert_type(w << 16, F32)
    hi = lax.bitcast_convert_type(w & jnp.uint32(0xFFFF0000), F32)
    return lo, hi


def peer_experts(x, gain, ids_flat, gates_t, u, v):
    m, d = x.shape
    assert m % GATE_W == 0 and d % 256 == 0, (m, d)
    n_blk = m // EXP_TB
    uv_tiles = _pack_expert_tables(u, v)
    ids_spec = lambda index_map: pl.BlockSpec((EXP_TB * PEER_SEL,), index_map, memory_space=pltpu.SMEM)
    return pl.pallas_call(
        _expert_kernel,
        grid=(n_blk,),
        in_specs=[ids_spec(lambda i: (i,)),
                  ids_spec(lambda i: (jnp.minimum(i + 1, n_blk - 1),)),
                  pl.BlockSpec((EXP_TB, d), lambda i: (i, 0)),
                  pl.BlockSpec((1, d), lambda i: (0, 0)),
                  pl.BlockSpec((PEER_SEL, GATE_W), lambda i: (0, i // (GATE_W // EXP_TB))),
                  pl.BlockSpec(memory_space=pl.ANY)],
        out_specs=pl.BlockSpec((EXP_TB, d), lambda i: (i, 0)),
        out_shape=jax.ShapeDtypeStruct((m, d), F32),
        scratch_shapes=[pltpu.VMEM((EXP_TILE_ROWS, 2 * d // 256, SUBLANES, 128), jnp.uint32)] * EXP_SLOTS + [
                        pltpu.VMEM((EXP_TB, d), F32),
                        pltpu.SemaphoreType.DMA((EXP_SLOTS,))],
        compiler_params=_params("arbitrary"),
        name="peer_experts",
    )(ids_flat, ids_flat, x, gain.reshape(1, d), gates_t, uv_tiles)


def _relayout_w_in(w):
    d = w.shape[0]
    z = lambda n: jnp.zeros((d, n), w.dtype)
    return jnp.concatenate([
        w[:, 0:2048],
        w[:, 2056:4104],
        w[:, 4104:5128],
        w[:, 5640:6664],
        w[:, 5128:5640],
        w[:, 2048:2056], z(120),
        w[:, 6664:6680], z(112),
        z(256)], axis=1)


def kernel(x, norm_mix, norm_ffn, norm_final, w_in, w_out, gdn_conv_w, gdn_a_log, gdn_dt_bias, gdn_norm,
           hgrn_lb_logits, hgrn_norm, lru_conv_w, lru_conv_b, lru_w_a, lru_b_a, lru_w_x, lru_b_x, lru_lambda,
           gla_w_gate, gla_b_gate, gla_norm, peer_w_q, peer_sub_keys, peer_u, peer_v):
    bsz, t, d = x.shape
    m = bsz * t
    depth = w_in.shape[0]
    cc = _chunk_consts()
    lb_all = jnp.cumsum(jax.nn.softmax(hgrn_lb_logits.astype(F32), axis=0), axis=0)
    lb_all = lb_all - lb_all[0]
    tm = 512 if m % 512 == 0 else 256
    xf = x.reshape(m, d)
    for l in range(depth):
        w_in_l = _relayout_w_in(w_in[l]).astype(BF16)
        tm_in = 2 * tm if m % (2 * tm) == 0 else tm
        proj = norm_matmul(xf, norm_mix[l], w_in_l, tm_in, P_WIDTH // 4).reshape(bsz, t, P_WIDTH)
        y_a = gdn_mixer(proj, gdn_conv_w[l], gdn_a_log[l], gdn_dt_bias[l], gdn_norm[l], cc)
        y_b = hgrn_mixer(proj, lb_all[l], hgrn_norm[l], cc)
        y_c = lru_mixer(proj, lru_conv_w[l], lru_conv_b[l], lru_w_a[l], lru_b_a[l], lru_w_x[l], lru_b_x[l],
                        lru_lambda[l])
        y_d = gla_mixer(proj, gla_w_gate[l], gla_b_gate[l], gla_norm[l], cc)
        ys = [y.reshape(m, GROUP_W) for y in (y_a, y_b, y_c, y_d)]
        xf = out_proj(ys, w_out[l].reshape(4, GROUP_W, d).astype(BF16), xf, tm)
        sk = peer_sub_keys[l].reshape(PEER_HEADS * 2, PEER_NKEYS, -1).astype(BF16)
        experts_t, gates_t = peer_route(xf, norm_ffn[l], peer_w_q[l].astype(BF16), sk)
        ids_flat = experts_t.T.reshape(m * PEER_SEL)
        xf = peer_experts(xf, norm_ffn[l], ids_flat, gates_t, peer_u[l], peer_v[l])
    return final_norm(xf, norm_final, tm).reshape(bsz, t, d)
```

```python
import functools
import math

import numpy as np
import jax
import jax.numpy as jnp
from jax import lax
from jax.experimental import pallas as pl
from jax.experimental.pallas import tpu as pltpu

F32 = jnp.float32
BF16 = jnp.bfloat16
EPS = 1e-6
CHUNK = 64
GROUP_W = 512
N_HEADS = 4
HEAD_DV = 128
GLA_DK = 64
GLA_RANK = 16
GLA_TAU = 16.0
LRU_C = 8.0
LRU_BLOCKS = 4
PEER_HEADS = 8
PEER_NKEYS = 128
PEER_TOPK = 16
PEER_SEL = PEER_HEADS * PEER_TOPK
NEG_BIG = -1e30
VMEM_LIMIT = 56 * 1024 * 1024

P_AQ, P_AK, P_AV, P_AZ, P_BQ, P_BF, P_BI, P_BG, P_CX, P_CG, P_DV, P_DG, P_DQK, P_SM = range(14)
P_BLOCKS = 14
P_WIDTH = P_BLOCKS * GROUP_W


def _dot(a, b):
    return jnp.dot(a, b, preferred_element_type=F32)


def _dot_nt(a, b):
    return lax.dot_general(a, b, (((1,), (1,)), ((), ())), preferred_element_type=F32)


def _split3(a):
    a1 = a.astype(BF16)
    r1 = a - a1.astype(F32)
    a2 = r1.astype(BF16)
    r2 = r1 - a2.astype(F32)
    return a1, a2, r2.astype(BF16)


def _mm3(a, b, nt=False):
    d = _dot_nt if nt else _dot
    a1, a2, _ = _split3(a)
    b1, b2, _ = _split3(b)
    return d(a1, b1) + (d(a1, b2) + d(a2, b1))


def _mm_exact_l(w_bf, x):
    x1, x2, x3 = _split3(x)
    return _dot(w_bf, x1) + (_dot(w_bf, x2) + _dot(w_bf, x3))


def _mm_exact_r(x, w_bf):
    x1, x2, x3 = _split3(x)
    return _dot(x1, w_bf) + (_dot(x2, w_bf) + _dot(x3, w_bf))


def _sigmoid(x):
    return 1.0 / (1.0 + jnp.exp(-x))


def _silu(x):
    return x * _sigmoid(x)


def _softplus(x):
    return jnp.maximum(x, 0.0) + jnp.log(1.0 + jnp.exp(-jnp.abs(x)))


def _gelu_tanh(x):
    return 0.5 * x * (1.0 + jnp.tanh(math.sqrt(2.0 / math.pi) * (x + 0.044715 * (x * x * x))))


def _rms(x, gain):
    return x * lax.rsqrt(jnp.mean(x * x, axis=-1, keepdims=True) + EPS) * gain


def _params(*sem):
    return pltpu.CompilerParams(dimension_semantics=sem, vmem_limit_bytes=VMEM_LIMIT)


def _norm_matmul_kernel(x_ref, g_ref, w_ref, o_ref, h_ref):
    @pl.when(pl.program_id(1) == 0)
    def _():
        h_ref[...] = _rms(x_ref[...], g_ref[...]).astype(BF16)

    o_ref[...] = _dot(h_ref[...], w_ref[...]).astype(o_ref.dtype)


def norm_matmul(x, gain, w_bf, tm, tn, out_dtype=F32):
    m, d = x.shape
    n = w_bf.shape[1]
    return pl.pallas_call(
        _norm_matmul_kernel,
        grid=(m // tm, n // tn),
        in_specs=[pl.BlockSpec((tm, d), lambda i, j: (i, 0)),
                  pl.BlockSpec((1, d), lambda i, j: (0, 0)),
                  pl.BlockSpec((d, tn), lambda i, j: (0, j))],
        out_specs=pl.BlockSpec((tm, tn), lambda i, j: (i, j)),
        out_shape=jax.ShapeDtypeStruct((m, n), out_dtype),
        scratch_shapes=[pltpu.VMEM((tm, d), BF16)],
        compiler_params=_params("parallel", "arbitrary"),
        name="norm_matmul",
    )(x, gain.reshape(1, d), w_bf)


def _out_proj_kernel(ya_ref, yb_ref, yc_ref, yd_ref, w_ref, x_ref, o_ref):
    acc = x_ref[...]
    for g, y_ref in enumerate((ya_ref, yb_ref, yc_ref, yd_ref)):
        acc = acc + _dot(y_ref[...], w_ref[g])
    o_ref[...] = acc


def out_proj(ys, w_bf, x, tm):
    m, d = x.shape
    y_spec = pl.BlockSpec((tm, GROUP_W), lambda i: (i, 0))
    return pl.pallas_call(
        _out_proj_kernel,
        grid=(m // tm,),
        in_specs=[y_spec, y_spec, y_spec, y_spec,
                  pl.BlockSpec((4, GROUP_W, d), lambda i: (0, 0, 0)),
                  pl.BlockSpec((tm, d), lambda i: (i, 0))],
        out_specs=pl.BlockSpec((tm, d), lambda i: (i, 0)),
        out_shape=jax.ShapeDtypeStruct((m, d), F32),
        compiler_params=_params("parallel"),
        name="out_proj",
    )(*ys, w_bf, x)


def _final_norm_kernel(x_ref, g_ref, o_ref):
    o_ref[...] = _rms(x_ref[...], g_ref[...])


def final_norm(x, gain, tm):
    m, d = x.shape
    return pl.pallas_call(
        _final_norm_kernel,
        grid=(m // tm,),
        in_specs=[pl.BlockSpec((tm, d), lambda i: (i, 0)), pl.BlockSpec((1, d), lambda i: (0, 0))],
        out_specs=pl.BlockSpec((tm, d), lambda i: (i, 0)),
        out_shape=jax.ShapeDtypeStruct((m, d), F32),
        compiler_params=_params("parallel"),
        name="final_norm",
    )(x, gain.reshape(1, d))


def _chunk_consts():
    i = np.arange(CHUNK)[:, None]
    t = np.arange(CHUNK)[None, :]
    tril = (t <= i).astype(np.float32)
    after = (t > i).astype(np.float32)
    levels, low, masks = [], [], []
    s = CHUNK // 2
    while s >= 1:
        pos = i % (2 * s)
        ref_row = (i // (2 * s)) * (2 * s) + s
        lower = pos >= s
        w = np.where(lower, (t > ref_row) & (t <= i), (t > i) & (t <= ref_row)).astype(np.float32)
        levels.append(w)
        low.append(np.broadcast_to(lower, (CHUNK, 128)).astype(np.float32))
        same = (i // (2 * s)) == (t // (2 * s))
        masks.append((same & lower & ((t % (2 * s)) < s)).astype(np.float32))
        s //= 2
    wall = np.concatenate([tril, after] + levels, axis=0)
    return (jnp.asarray(wall, BF16), jnp.asarray(np.stack(low), F32), jnp.asarray(np.stack(masks), F32),
            jnp.asarray(tril, BF16), jnp.asarray(tril.T, BF16))


N_LEVELS = int(math.log2(CHUNK))


def _gla_chunks(qs_all, k_all, v_all, lg_all, sts, dk, wall, low_ref, mask_ref):
    nh = len(sts)
    hs = lambda a, h, w: a[:, h * w:(h + 1) * w]
    e = _mm_exact_l(wall, lg_all)
    bcum = e[0:CHUNK]
    brev = e[CHUNK:2 * CHUNK]
    q_bf = qs_all.astype(BF16)
    k_bf = k_all.astype(BF16)
    eye = _eye()
    attn = [jnp.where(eye, _dot_nt(hs(q_bf, h, dk), hs(k_bf, h, dk)), 0.0) for h in range(nh)]
    for lv in range(N_LEVELS):
        f = jnp.exp(e[(2 + lv) * CHUNK:(3 + lv) * CHUNK])
        z = (jnp.where(low_ref[lv][:, 0:1] > 0.5, qs_all, k_all) * f).astype(BF16)
        keep = mask_ref[lv] > 0.5
        attn = [attn[h] + jnp.where(keep, _dot_nt(hs(z, h, dk), hs(z, h, dk)), 0.0) for h in range(nh)]
    qd = (qs_all * jnp.exp(bcum)).astype(BF16)
    kdec = (k_all * jnp.exp(brev)).astype(BF16)
    eb = jnp.exp(bcum[CHUNK - 1:CHUNK])
    v_bf = v_all.astype(BF16)
    outs = [_dot_nt(hs(qd, h, dk), sts[h].astype(BF16)) + _dot(attn[h].astype(BF16), hs(v_bf, h, HEAD_DV))
            for h in range(nh)]
    new = [sts[h] * hs(eb, h, dk) + _dot(hs(v_all, h, HEAD_DV).T.astype(BF16), hs(kdec, h, dk))
           for h in range(nh)]
    return outs, new


def _eye():
    r = lax.broadcasted_iota(jnp.int32, (CHUNK, CHUNK), 0)
    c = lax.broadcasted_iota(jnp.int32, (CHUNK, CHUNK), 1)
    return r == c


def _gated_out(o, gate, nw):
    o = o * lax.rsqrt(jnp.mean(o * o, axis=-1, keepdims=True) + EPS) * nw
    return o * _silu(gate)


def _hgrn_kernel(q_ref, f_ref, i_ref, g_ref, lb_ref, nw_ref, wall_ref, low_ref, mask_ref, o_ref, st_ref):
    @pl.when(pl.program_id(1) == 0)
    def _():
        st_ref[...] = jnp.zeros_like(st_ref)

    rows = lambda fn: jnp.concatenate([fn(bb) for bb in range(MIX_BB)], axis=1)
    lb = rows(lambda bb: lb_ref[...])
    sg = _sigmoid(rows(lambda bb: f_ref[bb]))
    lg_all = jnp.log(lb + (1.0 - lb) * sg)
    k_all = (1.0 - lb) * (1.0 - sg)
    q_all = _silu(rows(lambda bb: q_ref[bb])) * (HEAD_DV ** -0.5)
    v_all = rows(lambda bb: i_ref[bb])
    _gla_finish(q_all, k_all, v_all, lg_all, HEAD_DV, g_ref, nw_ref, wall_ref, low_ref, mask_ref, o_ref, st_ref)


def _gla_finish(q_all, k_all, v_all, lg_all, dk, g_ref, nw_ref, wall_ref, low_ref, mask_ref, o_ref, st_ref):
    pairs = [(bb, h) for bb in range(MIX_BB) for h in range(N_HEADS)]
    sts = [st_ref[bb, h] for bb, h in pairs]
    outs, new = _gla_chunks(q_all, k_all, v_all, lg_all, sts, dk, wall_ref[...], low_ref, mask_ref)
    for (bb, h), o, st_new in zip(pairs, outs, new):
        sl = slice(h * HEAD_DV, (h + 1) * HEAD_DV)
        st_ref[bb, h] = st_new
        o_ref[bb, :, sl] = _gated_out(o, g_ref[bb, :, sl], nw_ref[...]).astype(o_ref.dtype)


def _gla_kernel(qk_ref, v_ref, g_ref, sm_ref, wg_ref, bg_ref, nw_ref, wall_ref, low_ref, mask_ref, o_ref, st_ref):
    @pl.when(pl.program_id(1) == 0)
    def _():
        st_ref[...] = jnp.zeros_like(st_ref)

    rows = lambda fn: jnp.concatenate([fn(bb) for bb in range(MIX_BB)], axis=1)
    nq = N_HEADS * GLA_DK
    gate_in = rows(lambda bb: _mm3(sm_ref[bb], wg_ref[...]) + bg_ref[...])
    lg_all = -_softplus(-gate_in) * (1.0 / GLA_TAU)
    q_all = rows(lambda bb: qk_ref[bb, :, 0:nq]) * (GLA_DK ** -0.5)
    k_all = rows(lambda bb: qk_ref[bb, :, nq:2 * nq])
    v_all = rows(lambda bb: v_ref[bb])
    _gla_finish(q_all, k_all, v_all, lg_all, GLA_DK, g_ref, nw_ref, wall_ref, low_ref, mask_ref, o_ref, st_ref)


MIX_BB = 4


def _pspec(col_block, width=GROUP_W):
    return pl.BlockSpec((MIX_BB, CHUNK, width), lambda b, c: (b, c, col_block))


def _full(shape):
    nd = len(shape)
    return pl.BlockSpec(shape, lambda b, c: (0,) * nd)


def _per_batch_row(body, n_blocked, n_consts):
    def kern(*refs):
        blocked = refs[:n_blocked]
        consts = refs[n_blocked:n_blocked + n_consts]
        out = refs[n_blocked + n_consts]
        scratch = refs[n_blocked + n_consts + 1:]
        for bb in range(MIX_BB):
            body(*[r.at[bb] for r in blocked], *consts, out.at[bb], *[r.at[bb] for r in scratch])
    return kern


def _mixer_call(body, proj, col_specs, consts, scratch, name, per_row=True):
    bsz, t, _ = proj.shape
    assert bsz % MIX_BB == 0 and t % CHUNK == 0, (bsz, t)
    return pl.pallas_call(
        _per_batch_row(body, len(col_specs), len(consts)) if per_row else body,
        grid=(bsz // MIX_BB, t // CHUNK),
        in_specs=col_specs + [_full(c.shape) for c in consts],
        out_specs=pl.BlockSpec((MIX_BB, CHUNK, GROUP_W), lambda b, c: (b, c, 0)),
        out_shape=jax.ShapeDtypeStruct((bsz, t, GROUP_W), BF16),
        scratch_shapes=[pltpu.VMEM((MIX_BB,) + shape, F32) for shape in scratch],
        compiler_params=_params("parallel", "arbitrary"),
        name=name,
    )(*([proj] * len(col_specs)), *consts)


def hgrn_mixer(proj, lb, nw, cc):
    wall, low, masks, _, _ = cc
    return _mixer_call(
        _hgrn_kernel, proj, [_pspec(P_BQ), _pspec(P_BF), _pspec(P_BI), _pspec(P_BG)],
        [lb.reshape(1, GROUP_W), nw.reshape(1, HEAD_DV), wall, low, masks],
        [(N_HEADS, HEAD_DV, HEAD_DV)], "hgrn2_mixer", per_row=False)


def gla_mixer(proj, w_gate, b_gate, nw, cc):
    wall, low, masks, _, _ = cc
    wg = jnp.zeros((128, N_HEADS * GLA_DK), F32).at[0:GLA_RANK].set(w_gate)
    return _mixer_call(
        _gla_kernel, proj, [_pspec(P_DQK), _pspec(P_DV), _pspec(P_DG), _pspec(P_SM * 4 + 1, 128)],
        [wg, b_gate.reshape(1, N_HEADS * GLA_DK), nw.reshape(1, HEAD_DV), wall, low, masks],
        [(N_HEADS, HEAD_DV, GLA_DK)], "gla_mixer", per_row=False)


def _causal_conv(x_ref, tail_ref, slot, cw):
    cur = x_ref[...]
    ext = jnp.concatenate([tail_ref[slot], cur], axis=0)
    y = ext[5:5 + CHUNK] * cw[0:1]
    for w in range(1, 4):
        y = y + ext[5 + w:5 + w + CHUNK] * cw[w:w + 1]
    tail_ref[slot] = cur[CHUNK - 8:CHUNK]
    return y


def _l2n(t):
    return t * lax.rsqrt(jnp.sum(t * t, axis=-1, keepdims=True) + EPS)


def _gdn_kernel(q_ref, k_ref, v_ref, z_ref, sm_ref, cw_ref, alog_ref, dtb_ref, nw_ref, tril_ref, trilt_ref,
                o_ref, tail_ref, s_ref):
    @pl.when(pl.program_id(1) == 0)
    def _():
        tail_ref[...] = jnp.zeros_like(tail_ref)
        s_ref[...] = jnp.zeros_like(s_ref)

    r = lax.broadcasted_iota(jnp.int32, (CHUNK, CHUNK), 0)
    c = lax.broadcasted_iota(jnp.int32, (CHUNK, CHUNK), 1)
    causal = r >= c
    strict = r > c
    eye = (r == c).astype(F32)
    scale = HEAD_DV ** -0.5
    pairs = [(bb, h) for bb in range(MIX_BB) for h in range(N_HEADS)]
    each = lambda fn: [fn(i) for i in range(len(pairs))]
    q_rows, k_rows, v_rows, gcum, gcum_t, beta_rows = [], [], [], [], [], []
    for bb in range(MIX_BB):
        tails = tail_ref.at[bb]
        q_rows.append(_silu(_causal_conv(q_ref.at[bb], tails, 0, cw_ref[0])))
        k_rows.append(_silu(_causal_conv(k_ref.at[bb], tails, 1, cw_ref[1])))
        v_rows.append(_silu(_causal_conv(v_ref.at[bb], tails, 2, cw_ref[2])))
        sm = sm_ref[bb]
        g_full = -(jnp.exp(alog_ref[...]) * _softplus(sm + dtb_ref[...]))
        beta_rows.append(_sigmoid(sm))
        gcum.append(_mm_exact_l(tril_ref[...], g_full))
        gcum_t.append(_mm_exact_r(g_full.T, trilt_ref[...]))
    head = lambda rows, i: rows[pairs[i][0]][:, pairs[i][1] * HEAD_DV:(pairs[i][1] + 1) * HEAD_DV]
    qh = each(lambda i: _l2n(head(q_rows, i)) * scale)
    kh = each(lambda i: _l2n(head(k_rows, i)))
    vh = each(lambda i: head(v_rows, i))
    gc = each(lambda i: gcum[pairs[i][0]][:, pairs[i][1]:pairs[i][1] + 1])
    gl = each(lambda i: gc[i][CHUNK - 1:CHUNK])
    beta = each(lambda i: beta_rows[pairs[i][0]][:, N_HEADS + pairs[i][1]:N_HEADS + pairs[i][1] + 1])
    decay = each(lambda i: jnp.exp(jnp.where(
        causal, gc[i] - gcum_t[pairs[i][0]][pairs[i][1]:pairs[i][1] + 1, :], NEG_BIG)))
    kb = each(lambda i: kh[i] * beta[i])
    x = each(lambda i: -jnp.where(strict, _mm3(kb[i], kh[i], nt=True) * decay[i], 0.0))
    tinv = each(lambda i: eye + x[i])
    p = x
    for _ in range(N_LEVELS - 1):
        p = each(lambda i: _mm3(p[i], p[i]))
        tinv = each(lambda i: tinv[i] + _mm3(tinv[i], p[i]))
    egc = each(lambda i: jnp.exp(gc[i]))
    u = each(lambda i: _mm3(tinv[i], vh[i] * beta[i]))
    w = each(lambda i: _mm3(tinv[i], kb[i] * egc[i]))
    s = each(lambda i: s_ref[pairs[i][0], pairs[i][1]])
    v_new = each(lambda i: u[i] - _mm3(w[i], s[i]))
    bf = lambda a: a.astype(BF16)
    attn = each(lambda i: jnp.where(causal, _dot_nt(bf(qh[i]), bf(kh[i])) * decay[i], 0.0))
    o = each(lambda i: _dot(bf(qh[i] * egc[i]), bf(s[i])) + _dot(bf(attn[i]), bf(v_new[i])))
    s_new = each(lambda i: s[i] * jnp.exp(gl[i]) + _mm3((kh[i] * jnp.exp(gl[i] - gc[i])).T, v_new[i]))
    for i, (bb, h) in enumerate(pairs):
        sl = slice(h * HEAD_DV, (h + 1) * HEAD_DV)
        s_ref[bb, h] = s_new[i]
        o_ref[bb, :, sl] = _gated_out(o[i], z_ref[bb, :, sl], nw_ref[...]).astype(o_ref.dtype)


def gdn_mixer(proj, conv_w, a_log, dt_bias, nw, cc):
    _, _, _, tril, trilt = cc
    cw = jnp.zeros((3, 8, GROUP_W), F32).at[:, 0:4].set(conv_w.reshape(4, 3, GROUP_W).transpose(1, 0, 2))
    alog = jnp.zeros((1, 128), F32).at[0, 0:N_HEADS].set(a_log)
    dtb = jnp.zeros((1, 128), F32).at[0, 0:N_HEADS].set(dt_bias)
    return _mixer_call(
        _gdn_kernel, proj,
        [_pspec(P_AQ), _pspec(P_AK), _pspec(P_AV), _pspec(P_AZ), _pspec(P_SM * 4, 128)],
        [cw, alog, dtb, nw.reshape(1, HEAD_DV), tril, trilt],
        [(3, 8, GROUP_W), (N_HEADS, HEAD_DV, HEAD_DV)], "gdn_mixer", per_row=False)


def _shift_rows(x, d, fill):
    rows = lax.broadcasted_iota(jnp.int32, x.shape, 0)
    return jnp.where(rows >= d, pltpu.roll(x, d, 0), fill)


def _lru_kernel(x_ref, g_ref, cw_ref, cb_ref, wa_ref, ba_ref, wx_ref, bx_ref, lam_ref, o_ref, tail_ref, h_ref):
    @pl.when(pl.program_id(1) == 0)
    def _():
        tail_ref[...] = jnp.zeros_like(tail_ref)
        h_ref[...] = jnp.zeros_like(h_ref)

    xb = _causal_conv(x_ref, tail_ref, 0, cw_ref[...]) + cb_ref[...]
    xb_bf = xb.astype(BF16)
    bw = GROUP_W // LRU_BLOCKS
    ra = jnp.concatenate([_dot(xb_bf[:, g * bw:(g + 1) * bw], wa_ref[g]) for g in range(LRU_BLOCKS)], axis=1)
    rx = jnp.concatenate([_dot(xb_bf[:, g * bw:(g + 1) * bw], wx_ref[g]) for g in range(LRU_BLOCKS)], axis=1)
    rg = _sigmoid(ra + ba_ref[...])
    ig = _sigmoid(rx + bx_ref[...])
    log_a = -LRU_C * rg * _softplus(-lam_ref[...])
    a = jnp.exp(log_a)
    b = jnp.sqrt(1.0 - jnp.exp(2.0 * log_a)) * ig * xb
    d = 1
    while d < CHUNK:
        b = a * _shift_rows(b, d, 0.0) + b
        a = a * _shift_rows(a, d, 1.0)
        d *= 2
    h = b + a * h_ref[...]
    h_ref[...] = h[CHUNK - 1:CHUNK]
    o_ref[...] = (h * _gelu_tanh(g_ref[...])).astype(o_ref.dtype)


def lru_mixer(proj, conv_w, conv_b, w_a, b_a, w_x, b_x, lam):
    cw = jnp.zeros((8, GROUP_W), F32).at[0:4].set(conv_w)
    row = lambda v: v.reshape(1, GROUP_W)
    return _mixer_call(
        _lru_kernel, proj, [_pspec(P_CX), _pspec(P_CG)],
        [cw, row(conv_b), w_a.astype(BF16), row(b_a), w_x.astype(BF16), row(b_x), row(lam)],
        [(1, 8, GROUP_W), (1, GROUP_W)], "rglru_mixer")


ROUTE_TB = 256


def _topk_rows(s, k):
    n = s.shape[0]
    rows = lax.broadcasted_iota(jnp.int32, s.shape, 0)
    vals, ids = [], []
    for _ in range(k):
        m = jnp.max(s, axis=0, keepdims=True)
        idx = jnp.min(jnp.where(s == m, rows, n), axis=0, keepdims=True)
        vals.append(m)
        ids.append(idx)
        s = jnp.where(rows == idx, NEG_BIG, s)
    return jnp.concatenate(vals, axis=0), jnp.concatenate(ids, axis=0)


def _route_kernel(x_ref, g_ref, wq_ref, sk_ref, ex_ref, gt_ref):
    h2 = _rms(x_ref[...], g_ref[...]).astype(BF16)
    q = _dot(h2, wq_ref[...])
    for h in range(PEER_HEADS):
        tops = []
        for p in range(2):
            col = (h * 2 + p) * PEER_NKEYS
            sc = _dot_nt(sk_ref[h * 2 + p], q[:, col:col + PEER_NKEYS].astype(BF16))
            tops.append(_topk_rows(sc, PEER_TOPK))
        (s1, i1), (s2, i2) = tops
        sub = lax.broadcasted_iota(jnp.int32, (8, s1.shape[1]), 0)
        half = PEER_TOPK // 2
        cs = [s1[0:1] + s2[0:half], s1[0:1] + s2[half:PEER_TOPK]]
        ci = [i1[0:1] * PEER_NKEYS + i2[0:half], i1[0:1] * PEER_NKEYS + i2[half:PEER_TOPK]]
        for a in range(1, half):
            n_valid = PEER_TOPK // (a + 1)
            cs.append(jnp.where(sub < n_valid, s1[a:a + 1] + s2[0:half], NEG_BIG))
            ci.append(i1[a:a + 1] * PEER_NKEYS + i2[0:half])
        cs.append(s1[half:PEER_TOPK] + s2[0:1])
        ci.append(i1[half:PEER_TOPK] * PEER_NKEYS + i2[0:1])
        cand_s = jnp.concatenate(cs, axis=0)
        cand_i = jnp.concatenate(ci, axis=0)
        best_s, best_pos = _topk_rows(cand_s, PEER_TOPK)
        pos = lax.broadcasted_iota(jnp.int32, cand_s.shape, 0)
        experts = jnp.concatenate(
            [jnp.sum(jnp.where(pos == best_pos[r:r + 1], cand_i, 0), axis=0, keepdims=True)
             for r in range(PEER_TOPK)], axis=0)
        ew = jnp.exp(best_s - best_s[0:1])
        gates = ew / jnp.sum(ew, axis=0, keepdims=True)
        ex_ref[h * PEER_TOPK:(h + 1) * PEER_TOPK, :] = experts
        gt_ref[h * PEER_TOPK:(h + 1) * PEER_TOPK, :] = gates


def peer_route(x, gain, wq_bf, sk_bf):
    m, d = x.shape
    nq = wq_bf.shape[1]
    assert m % ROUTE_TB == 0, m
    return pl.pallas_call(
        _route_kernel,
        grid=(m // ROUTE_TB,),
        in_specs=[pl.BlockSpec((ROUTE_TB, d), lambda i: (i, 0)),
                  pl.BlockSpec((1, d), lambda i: (0, 0)),
                  pl.BlockSpec((d, nq), lambda i: (0, 0)),
                  pl.BlockSpec(sk_bf.shape, lambda i: (0, 0, 0))],
        out_specs=[pl.BlockSpec((PEER_SEL, ROUTE_TB), lambda i: (0, i)),
                   pl.BlockSpec((PEER_SEL, ROUTE_TB), lambda i: (0, i))],
        out_shape=[jax.ShapeDtypeStruct((PEER_SEL, m), jnp.int32),
                   jax.ShapeDtypeStruct((PEER_SEL, m), F32)],
        compiler_params=_params("parallel"),
        name="peer_route",
    )(x, gain.reshape(1, d), wq_bf, sk_bf)


EXP_TB = 64
GATE_W = 128
EXP_TG = 8
EXP_SLOTS = 4
EXP_AHEAD = 2
SUBLANES = 8
EXP_ROWS = EXP_TG * PEER_SEL
EXP_TILE_ROWS = EXP_ROWS // SUBLANES
TOK_TILE_ROWS = PEER_SEL // SUBLANES


def _expert_kernel(ids_ref, nxt_ref, x_ref, g_ref, gt_ref, uv_ref, o_ref, *scratch):
    bufs, (h_ref, sem) = scratch[:EXP_SLOTS], scratch[EXP_SLOTS:]
    d = x_ref.shape[1]
    n_word = d // 256
    n_sub = EXP_TB // EXP_TG
    step = pl.program_id(0)
    lane0 = (step % (GATE_W // EXP_TB)) * EXP_TB
    rows_per_iter = EXP_TILE_ROWS // (2 * EXP_TG)
    h_ref[...] = _rms(x_ref[...], g_ref[...])

    def issue_tile_row(src_ids, sub, slot, i):
        base = sub * EXP_ROWS + i * SUBLANES
        for k in range(SUBLANES):
            e = src_ids[base + k]
            pltpu.make_async_copy(uv_ref.at[e], bufs[slot].at[i, :, k, :], sem.at[slot]).start(priority=k % 2)

    def issue_all(src_ids, sub, slot):
        def body(i, carry):
            issue_tile_row(src_ids, sub, slot, i)
            return carry
        lax.fori_loop(0, EXP_TILE_ROWS, body, 0)

    def wait(slot):
        pltpu.make_async_copy(bufs[(slot + 1) % EXP_SLOTS], bufs[slot], sem.at[slot]).wait()

    lane = lax.broadcasted_iota(jnp.int32, (PEER_SEL, GATE_W), 1)

    def compute(sub, slot, ahead):
        def request(first_row):
            if ahead is not None:
                for ii in range(rows_per_iter):
                    issue_tile_row(ahead[0], ahead[1], ahead[2], first_row + ii)

        def u_side(t, act):
            n = sub * EXP_TG + t
            request(t * rows_per_iter)
            xn = h_ref[pl.ds(n, 1), :]
            parts = []
            for g in range(TOK_TILE_ROWS):
                acc = None
                for c in range(n_word):
                    lo, hi = _unpack_pair(bufs[slot][t * TOK_TILE_ROWS + g, c])
                    term = (lo * xn[:, c * 128:(c + 1) * 128]
                            + hi * xn[:, d // 2 + c * 128:d // 2 + (c + 1) * 128])
                    acc = term if acc is None else acc + term
                parts.append(acc)
            s_col = jnp.sum(jnp.concatenate(parts, axis=0), axis=-1, keepdims=True)
            return jnp.where(lane == lane0 + n, s_col, act)

        act = lax.fori_loop(0, EXP_TG, u_side, jnp.zeros((PEER_SEL, GATE_W), F32))
        coeff = gt_ref[...] * _gelu_tanh(act)

        def v_side(t, carry):
            n = sub * EXP_TG + t
            request((EXP_TG + t) * rows_per_iter)
            c_col = jnp.sum(jnp.where(lane == lane0 + n, coeff, 0.0), axis=-1, keepdims=True)
            c_b = jnp.broadcast_to(c_col, (PEER_SEL, 128))
            outs_lo, outs_hi = [], []
            for c in range(n_word):
                acc_lo = acc_hi = None
                for g in range(TOK_TILE_ROWS):
                    lo, hi = _unpack_pair(bufs[slot][t * TOK_TILE_ROWS + g, n_word + c])
                    cg = c_b[g * SUBLANES:(g + 1) * SUBLANES]
                    acc_lo = lo * cg if acc_lo is None else acc_lo + lo * cg
                    acc_hi = hi * cg if acc_hi is None else acc_hi + hi * cg
                outs_lo.append(jnp.sum(acc_lo, axis=0, keepdims=True))
                outs_hi.append(jnp.sum(acc_hi, axis=0, keepdims=True))
            o_ref[pl.ds(n, 1), :] = x_ref[pl.ds(n, 1), :] + jnp.concatenate(outs_lo + outs_hi, axis=1)
            return carry

        lax.fori_loop(0, EXP_TG, v_side, 0)

    @pl.when(step == 0)
    def _():
        for s in range(EXP_AHEAD):
            issue_all(ids_ref, s, s)

    def group(q, carry):
        for r in range(EXP_SLOTS):
            sub = q * EXP_SLOTS + r
            wait(r)
            nxt_slot = (r + EXP_AHEAD) % EXP_SLOTS
            if r + EXP_AHEAD < EXP_SLOTS:
                compute(sub, r, (ids_ref, sub + EXP_AHEAD, nxt_slot))
            else:
                in_step = sub + EXP_AHEAD < n_sub
                pl.when(in_step)(functools.partial(compute, sub, r, (ids_ref, sub + EXP_AHEAD, nxt_slot)))
                pl.when(jnp.logical_not(in_step))(
                    functools.partial(compute, sub, r, (nxt_ref, r + EXP_AHEAD - EXP_SLOTS, nxt_slot)))
        return carry

    lax.fori_loop(0, n_sub // EXP_SLOTS, group, 0)

    @pl.when(step + 1 == pl.num_programs(0))
    def _():
        for s in range(EXP_AHEAD):
            wait(s)


def _pack_bf16_pairs(t):
    e, d = t.shape
    bits = lax.bitcast_convert_type(t.astype(BF16), jnp.uint16).astype(jnp.uint32)
    words = bits[:, :d // 2] | (bits[:, d // 2:] << 16)
    return words.reshape(e, d // 256, 128)


def _unpack_pair(w):
    lo = lax.bitcast_convert_type(w << 16, F32)
    hi = lax.bitcast_convert_type(w & jnp.uint32(0xFFFF0000), F32)
    return lo, hi


def peer_experts(x, gain, ids_flat, gates_t, u, v):
    m, d = x.shape
    assert m % GATE_W == 0 and d % 256 == 0, (m, d)
    n_blk = m // EXP_TB
    uv_tiles = jnp.concatenate([_pack_bf16_pairs(u), _pack_bf16_pairs(v)], axis=1)
    ids_spec = lambda index_map: pl.BlockSpec((EXP_TB * PEER_SEL,), index_map, memory_space=pltpu.SMEM)
    return pl.pallas_call(
        _expert_kernel,
        grid=(n_blk,),
        in_specs=[ids_spec(lambda i: (i,)),
                  ids_spec(lambda i: (jnp.minimum(i + 1, n_blk - 1),)),
                  pl.BlockSpec((EXP_TB, d), lambda i: (i, 0)),
                  pl.BlockSpec((1, d), lambda i: (0, 0)),
                  pl.BlockSpec((PEER_SEL, GATE_W), lambda i: (0, i // (GATE_W // EXP_TB))),
                  pl.BlockSpec(memory_space=pl.ANY)],
        out_specs=pl.BlockSpec((EXP_TB, d), lambda i: (i, 0)),
        out_shape=jax.ShapeDtypeStruct((m, d), F32),
        scratch_shapes=[pltpu.VMEM((EXP_TILE_ROWS, 2 * d // 256, SUBLANES, 128), jnp.uint32)] * EXP_SLOTS + [
                        pltpu.VMEM((EXP_TB, d), F32),
                        pltpu.SemaphoreType.DMA((EXP_SLOTS,))],
        compiler_params=_params("arbitrary"),
        name="peer_experts",
    )(ids_flat, ids_flat, x, gain.reshape(1, d), gates_t, uv_tiles)


def _relayout_w_in(w):
    d = w.shape[0]
    z = lambda n: jnp.zeros((d, n), w.dtype)
    return jnp.concatenate([
        w[:, 0:2048],
        w[:, 2056:4104],
        w[:, 4104:5128],
        w[:, 5640:6664],
        w[:, 5128:5640],
        w[:, 2048:2056], z(120),
        w[:, 6664:6680], z(112),
        z(256)], axis=1)


def kernel(x, norm_mix, norm_ffn, norm_final, w_in, w_out, gdn_conv_w, gdn_a_log, gdn_dt_bias, gdn_norm,
           hgrn_lb_logits, hgrn_norm, lru_conv_w, lru_conv_b, lru_w_a, lru_b_a, lru_w_x, lru_b_x, lru_lambda,
           gla_w_gate, gla_b_gate, gla_norm, peer_w_q, peer_sub_keys, peer_u, peer_v):
    bsz, t, d = x.shape
    m = bsz * t
    depth = w_in.shape[0]
    cc = _chunk_consts()
    lb_all = jnp.cumsum(jax.nn.softmax(hgrn_lb_logits.astype(F32), axis=0), axis=0)
    lb_all = lb_all - lb_all[0]
    tm = 512 if m % 512 == 0 else 256
    xf = x.reshape(m, d)
    for l in range(depth):
        w_in_l = _relayout_w_in(w_in[l]).astype(BF16)
        tm_in = 2 * tm if m % (2 * tm) == 0 else tm
        proj = norm_matmul(xf, norm_mix[l], w_in_l, tm_in, P_WIDTH // 4).reshape(bsz, t, P_WIDTH)
        y_a = gdn_mixer(proj, gdn_conv_w[l], gdn_a_log[l], gdn_dt_bias[l], gdn_norm[l], cc)
        y_b = hgrn_mixer(proj, lb_all[l], hgrn_norm[l], cc)
        y_c = lru_mixer(proj, lru_conv_w[l], lru_conv_b[l], lru_w_a[l], lru_b_a[l], lru_w_x[l], lru_b_x[l],
                        lru_lambda[l])
        y_d = gla_mixer(proj, gla_w_gate[l], gla_b_gate[l], gla_norm[l], cc)
        ys = [y.reshape(m, GROUP_W) for y in (y_a, y_b, y_c, y_d)]
        xf = out_proj(ys, w_out[l].reshape(4, GROUP_W, d).astype(BF16), xf, tm)
        sk = peer_sub_keys[l].reshape(PEER_HEADS * 2, PEER_NKEYS, -1).astype(BF16)
        experts_t, gates_t = peer_route(xf, norm_ffn[l], peer_w_q[l].astype(BF16), sk)
        ids_flat = experts_t.T.reshape(m * PEER_SEL)
        xf = peer_experts(xf, norm_ffn[l], ids_flat, gates_t, peer_u[l], peer_v[l])
    return final_norm(xf, norm_final, tm).reshape(bsz, t, d)
```

```python
import functools
import math

import numpy as np
import jax
import jax.numpy as jnp
from jax import lax
from jax.experimental import pallas as pl
from jax.experimental.pallas import tpu as pltpu

F32 = jnp.float32
BF16 = jnp.bfloat16
EPS = 1e-6
CHUNK = 64
GROUP_W = 512
N_HEADS = 4
HEAD_DV = 128
GLA_DK = 64
GLA_RANK = 16
GLA_TAU = 16.0
LRU_C = 8.0
LRU_BLOCKS = 4
PEER_HEADS = 8
PEER_NKEYS = 128
PEER_TOPK = 16
PEER_SEL = PEER_HEADS * PEER_TOPK
NEG_BIG = -1e30
VMEM_LIMIT = 56 * 1024 * 1024

P_AQ, P_AK, P_AV, P_AZ, P_BQ, P_BF, P_BI, P_BG, P_CX, P_CG, P_DV, P_DG, P_DQK, P_SM = range(14)
P_BLOCKS = 14
P_WIDTH = P_BLOCKS * GROUP_W


def _dot(a, b):
    return jnp.dot(a, b, preferred_element_type=F32)


def _dot_nt(a, b):
    return lax.dot_general(a, b, (((1,), (1,)), ((), ())), preferred_element_type=F32)


def _split3(a):
    a1 = a.astype(BF16)
    r1 = a - a1.astype(F32)
    a2 = r1.astype(BF16)
    r2 = r1 - a2.astype(F32)
    return a1, a2, r2.astype(BF16)


def _mm3(a, b, nt=False):
    d = _dot_nt if nt else _dot
    a1, a2, _ = _split3(a)
    b1, b2, _ = _split3(b)
    return d(a1, b1) + (d(a1, b2) + d(a2, b1))


def _mm_exact_l(w_bf, x):
    x1, x2, x3 = _split3(x)
    return _dot(w_bf, x1) + (_dot(w_bf, x2) + _dot(w_bf, x3))


def _mm_exact_r(x, w_bf):
    x1, x2, x3 = _split3(x)
    return _dot(x1, w_bf) + (_dot(x2, w_bf) + _dot(x3, w_bf))


def _sigmoid(x):
    return 1.0 / (1.0 + jnp.exp(-x))


def _silu(x):
    return x * _sigmoid(x)


def _softplus(x):
    return jnp.maximum(x, 0.0) + jnp.log(1.0 + jnp.exp(-jnp.abs(x)))


def _gelu_tanh(x):
    return 0.5 * x * (1.0 + jnp.tanh(math.sqrt(2.0 / math.pi) * (x + 0.044715 * (x * x * x))))


def _rms(x, gain):
    return x * lax.rsqrt(jnp.mean(x * x, axis=-1, keepdims=True) + EPS) * gain


def _params(*sem):
    return pltpu.CompilerParams(dimension_semantics=sem, vmem_limit_bytes=VMEM_LIMIT)


def _norm_matmul_kernel(x_ref, g_ref, w_ref, o_ref, h_ref):
    @pl.when(pl.program_id(1) == 0)
    def _():
        h_ref[...] = _rms(x_ref[...], g_ref[...]).astype(BF16)

    o_ref[...] = _dot(h_ref[...], w_ref[...]).astype(o_ref.dtype)


def norm_matmul(x, gain, w_bf, tm, tn, out_dtype=F32):
    m, d = x.shape
    n = w_bf.shape[1]
    return pl.pallas_call(
        _norm_matmul_kernel,
        grid=(m // tm, n // tn),
        in_specs=[pl.BlockSpec((tm, d), lambda i, j: (i, 0)),
                  pl.BlockSpec((1, d), lambda i, j: (0, 0)),
                  pl.BlockSpec((d, tn), lambda i, j: (0, j))],
        out_specs=pl.BlockSpec((tm, tn), lambda i, j: (i, j)),
        out_shape=jax.ShapeDtypeStruct((m, n), out_dtype),
        scratch_shapes=[pltpu.VMEM((tm, d), BF16)],
        compiler_params=_params("parallel", "arbitrary"),
        name="norm_matmul",
    )(x, gain.reshape(1, d), w_bf)


def _out_proj_kernel(ya_ref, yb_ref, yc_ref, yd_ref, w_ref, x_ref, o_ref):
    acc = x_ref[...]
    for g, y_ref in enumerate((ya_ref, yb_ref, yc_ref, yd_ref)):
        acc = acc + _dot(y_ref[...], w_ref[g])
    o_ref[...] = acc


def out_proj(ys, w_bf, x, tm):
    m, d = x.shape
    y_spec = pl.BlockSpec((tm, GROUP_W), lambda i: (i, 0))
    return pl.pallas_call(
        _out_proj_kernel,
        grid=(m // tm,),
        in_specs=[y_spec, y_spec, y_spec, y_spec,
                  pl.BlockSpec((4, GROUP_W, d), lambda i: (0, 0, 0)),
                  pl.BlockSpec((tm, d), lambda i: (i, 0))],
        out_specs=pl.BlockSpec((tm, d), lambda i: (i, 0)),
        out_shape=jax.ShapeDtypeStruct((m, d), F32),
        compiler_params=_params("parallel"),
        name="out_proj",
    )(*ys, w_bf, x)


def _final_norm_kernel(x_ref, g_ref, o_ref):
    o_ref[...] = _rms(x_ref[...], g_ref[...])


def final_norm(x, gain, tm):
    m, d = x.shape
    return pl.pallas_call(
        _final_norm_kernel,
        grid=(m // tm,),
        in_specs=[pl.BlockSpec((tm, d), lambda i: (i, 0)), pl.BlockSpec((1, d), lambda i: (0, 0))],
        out_specs=pl.BlockSpec((tm, d), lambda i: (i, 0)),
        out_shape=jax.ShapeDtypeStruct((m, d), F32),
        compiler_params=_params("parallel"),
        name="final_norm",
    )(x, gain.reshape(1, d))


def _chunk_consts():
    i = np.arange(CHUNK)[:, None]
    t = np.arange(CHUNK)[None, :]
    tril = (t <= i).astype(np.float32)
    after = (t > i).astype(np.float32)
    levels, low, masks = [], [], []
    s = CHUNK // 2
    while s >= 1:
        pos = i % (2 * s)
        ref_row = (i // (2 * s)) * (2 * s) + s
        lower = pos >= s
        w = np.where(lower, (t > ref_row) & (t <= i), (t > i) & (t <= ref_row)).astype(np.float32)
        levels.append(w)
        low.append(np.broadcast_to(lower, (CHUNK, 128)).astype(np.float32))
        same = (i // (2 * s)) == (t // (2 * s))
        masks.append((same & lower & ((t % (2 * s)) < s)).astype(np.float32))
        s //= 2
    wall = np.concatenate([tril, after] + levels, axis=0)
    return (jnp.asarray(wall, BF16), jnp.asarray(np.stack(low), F32), jnp.asarray(np.stack(masks), F32),
            jnp.asarray(tril, BF16), jnp.asarray(tril.T, BF16))


N_LEVELS = int(math.log2(CHUNK))


def _gla_chunks(qs_all, k_all, v_all, lg_all, sts, dk, wall, low_ref, mask_ref):
    nh = len(sts)
    hs = lambda a, h, w: a[:, h * w:(h + 1) * w]
    e = _mm_exact_l(wall, lg_all)
    bcum = e[0:CHUNK]
    brev = e[CHUNK:2 * CHUNK]
    q_bf = qs_all.astype(BF16)
    k_bf = k_all.astype(BF16)
    eye = _eye()
    attn = [jnp.where(eye, _dot_nt(hs(q_bf, h, dk), hs(k_bf, h, dk)), 0.0) for h in range(nh)]
    for lv in range(N_LEVELS):
        f = jnp.exp(e[(2 + lv) * CHUNK:(3 + lv) * CHUNK])
        z = (jnp.where(low_ref[lv][:, 0:1] > 0.5, qs_all, k_all) * f).astype(BF16)
        keep = mask_ref[lv] > 0.5
        attn = [attn[h] + jnp.where(keep, _dot_nt(hs(z, h, dk), hs(z, h, dk)), 0.0) for h in range(nh)]
    qd = (qs_all * jnp.exp(bcum)).astype(BF16)
    kdec = (k_all * jnp.exp(brev)).astype(BF16)
    eb = jnp.exp(bcum[CHUNK - 1:CHUNK])
    v_bf = v_all.astype(BF16)
    outs = [_dot_nt(hs(qd, h, dk), sts[h].astype(BF16)) + _dot(attn[h].astype(BF16), hs(v_bf, h, HEAD_DV))
            for h in range(nh)]
    new = [sts[h] * hs(eb, h, dk) + _dot(hs(v_all, h, HEAD_DV).T.astype(BF16), hs(kdec, h, dk))
           for h in range(nh)]
    return outs, new


def _eye():
    r = lax.broadcasted_iota(jnp.int32, (CHUNK, CHUNK), 0)
    c = lax.broadcasted_iota(jnp.int32, (CHUNK, CHUNK), 1)
    return r == c


def _gated_out(o, gate, nw):
    o = o * lax.rsqrt(jnp.mean(o * o, axis=-1, keepdims=True) + EPS) * nw
    return o * _silu(gate)


def _hgrn_kernel(q_ref, f_ref, i_ref, g_ref, lb_ref, nw_ref, wall_ref, low_ref, mask_ref, o_ref, st_ref):
    @pl.when(pl.program_id(1) == 0)
    def _():
        st_ref[...] = jnp.zeros_like(st_ref)

    rows = lambda fn: jnp.concatenate([fn(bb) for bb in range(MIX_BB)], axis=1)
    lb = rows(lambda bb: lb_ref[...])
    sg = _sigmoid(rows(lambda bb: f_ref[bb]))
    lg_all = jnp.log(lb + (1.0 - lb) * sg)
    k_all = (1.0 - lb) * (1.0 - sg)
    q_all = _silu(rows(lambda bb: q_ref[bb])) * (HEAD_DV ** -0.5)
    v_all = rows(lambda bb: i_ref[bb])
    _gla_finish(q_all, k_all, v_all, lg_all, HEAD_DV, g_ref, nw_ref, wall_ref, low_ref, mask_ref, o_ref, st_ref)


def _gla_finish(q_all, k_all, v_all, lg_all, dk, g_ref, nw_ref, wall_ref, low_ref, mask_ref, o_ref, st_ref):
    pairs = [(bb, h) for bb in range(MIX_BB) for h in range(N_HEADS)]
    sts = [st_ref[bb, h] for bb, h in pairs]
    outs, new = _gla_chunks(q_all, k_all, v_all, lg_all, sts, dk, wall_ref[...], low_ref, mask_ref)
    for (bb, h), o, st_new in zip(pairs, outs, new):
        sl = slice(h * HEAD_DV, (h + 1) * HEAD_DV)
        st_ref[bb, h] = st_new
        o_ref[bb, :, sl] = _gated_out(o, g_ref[bb, :, sl], nw_ref[...]).astype(o_ref.dtype)


def _gla_kernel(qk_ref, v_ref, g_ref, sm_ref, wg_ref, bg_ref, nw_ref, wall_ref, low_ref, mask_ref, o_ref, st_ref):
    @pl.when(pl.program_id(1) == 0)
    def _():
        st_ref[...] = jnp.zeros_like(st_ref)

    rows = lambda fn: jnp.concatenate([fn(bb) for bb in range(MIX_BB)], axis=1)
    nq = N_HEADS * GLA_DK
    gate_in = rows(lambda bb: _mm3(sm_ref[bb], wg_ref[...]) + bg_ref[...])
    lg_all = -_softplus(-gate_in) * (1.0 / GLA_TAU)
    q_all = rows(lambda bb: qk_ref[bb, :, 0:nq]) * (GLA_DK ** -0.5)
    k_all = rows(lambda bb: qk_ref[bb, :, nq:2 * nq])
    v_all = rows(lambda bb: v_ref[bb])
    _gla_finish(q_all, k_all, v_all, lg_all, GLA_DK, g_ref, nw_ref, wall_ref, low_ref, mask_ref, o_ref, st_ref)


MIX_BB = 4


def _pspec(col_block, width=GROUP_W):
    return pl.BlockSpec((MIX_BB, CHUNK, width), lambda b, c: (b, c, col_block))


def _full(shape):
    nd = len(shape)
    return pl.BlockSpec(shape, lambda b, c: (0,) * nd)


def _per_batch_row(body, n_blocked, n_consts):
    def kern(*refs):
        blocked = refs[:n_blocked]
        consts = refs[n_blocked:n_blocked + n_consts]
        out = refs[n_blocked + n_consts]
        scratch = refs[n_blocked + n_consts + 1:]
        for bb in range(MIX_BB):
            body(*[r.at[bb] for r in blocked], *consts, out.at[bb], *[r.at[bb] for r in scratch])
    return kern


def _mixer_call(body, proj, col_specs, consts, scratch, name, per_row=True):
    bsz, t, _ = proj.shape
    assert bsz % MIX_BB == 0 and t % CHUNK == 0, (bsz, t)
    return pl.pallas_call(
        _per_batch_row(body, len(col_specs), len(consts)) if per_row else body,
        grid=(bsz // MIX_BB, t // CHUNK),
        in_specs=col_specs + [_full(c.shape) for c in consts],
        out_specs=pl.BlockSpec((MIX_BB, CHUNK, GROUP_W), lambda b, c: (b, c, 0)),
        out_shape=jax.ShapeDtypeStruct((bsz, t, GROUP_W), BF16),
        scratch_shapes=[pltpu.VMEM((MIX_BB,) + shape, F32) for shape in scratch],
        compiler_params=_params("parallel", "arbitrary"),
        name=name,
    )(*([proj] * len(col_specs)), *consts)


def hgrn_mixer(proj, lb, nw, cc):
    wall, low, masks, _, _ = cc
    return _mixer_call(
        _hgrn_kernel, proj, [_pspec(P_BQ), _pspec(P_BF), _pspec(P_BI), _pspec(P_BG)],
        [lb.reshape(1, GROUP_W), nw.reshape(1, HEAD_DV), wall, low, masks],
        [(N_HEADS, HEAD_DV, HEAD_DV)], "hgrn2_mixer", per_row=False)


def gla_mixer(proj, w_gate, b_gate, nw, cc):
    wall, low, masks, _, _ = cc
    wg = jnp.zeros((128, N_HEADS * GLA_DK), F32).at[0:GLA_RANK].set(w_gate)
    return _mixer_call(
        _gla_kernel, proj, [_pspec(P_DQK), _pspec(P_DV), _pspec(P_DG), _pspec(P_SM * 4 + 1, 128)],
        [wg, b_gate.reshape(1, N_HEADS * GLA_DK), nw.reshape(1, HEAD_DV), wall, low, masks],
        [(N_HEADS, HEAD_DV, GLA_DK)], "gla_mixer", per_row=False)


def _causal_conv(x_ref, tail_ref, slot, cw):
    cur = x_ref[...]
    ext = jnp.concatenate([tail_ref[slot], cur], axis=0)
    y = ext[5:5 + CHUNK] * cw[0:1]
    for w in range(1, 4):
        y = y + ext[5 + w:5 + w + CHUNK] * cw[w:w + 1]
    tail_ref[slot] = cur[CHUNK - 8:CHUNK]
    return y


def _l2n(t):
    return t * lax.rsqrt(jnp.sum(t * t, axis=-1, keepdims=True) + EPS)


def _gdn_kernel(q_ref, k_ref, v_ref, z_ref, sm_ref, cw_ref, alog_ref, dtb_ref, nw_ref, tril_ref, trilt_ref,
                o_ref, tail_ref, s_ref):
    @pl.when(pl.program_id(1) == 0)
    def _():
        tail_ref[...] = jnp.zeros_like(tail_ref)
        s_ref[...] = jnp.zeros_like(s_ref)

    r = lax.broadcasted_iota(jnp.int32, (CHUNK, CHUNK), 0)
    c = lax.broadcasted_iota(jnp.int32, (CHUNK, CHUNK), 1)
    causal = r >= c
    strict = r > c
    eye = (r == c).astype(F32)
    scale = HEAD_DV ** -0.5
    pairs = [(bb, h) for bb in range(MIX_BB) for h in range(N_HEADS)]
    each = lambda fn: [fn(i) for i in range(len(pairs))]
    q_rows, k_rows, v_rows, gcum, gcum_t, beta_rows = [], [], [], [], [], []
    for bb in range(MIX_BB):
        tails = tail_ref.at[bb]
        q_rows.append(_silu(_causal_conv(q_ref.at[bb], tails, 0, cw_ref[0])))
        k_rows.append(_silu(_causal_conv(k_ref.at[bb], tails, 1, cw_ref[1])))
        v_rows.append(_silu(_causal_conv(v_ref.at[bb], tails, 2, cw_ref[2])))
        sm = sm_ref[bb]
        g_full = -(jnp.exp(alog_ref[...]) * _softplus(sm + dtb_ref[...]))
        beta_rows.append(_sigmoid(sm))
        gcum.append(_mm_exact_l(tril_ref[...], g_full))
        gcum_t.append(_mm_exact_r(g_full.T, trilt_ref[...]))
    head = lambda rows, i: rows[pairs[i][0]][:, pairs[i][1] * HEAD_DV:(pairs[i][1] + 1) * HEAD_DV]
    qh = each(lambda i: _l2n(head(q_rows, i)) * scale)
    kh = each(lambda i: _l2n(head(k_rows, i)))
    vh = each(lambda i: head(v_rows, i))
    gc = each(lambda i: gcum[pairs[i][0]][:, pairs[i][1]:pairs[i][1] + 1])
    gl = each(lambda i: gc[i][CHUNK - 1:CHUNK])
    beta = each(lambda i: beta_rows[pairs[i][0]][:, N_HEADS + pairs[i][1]:N_HEADS + pairs[i][1] + 1])
    decay = each(lambda i: jnp.exp(jnp.where(
        causal, gc[i] - gcum_t[pairs[i][0]][pairs[i][1]:pairs[i][1] + 1, :], NEG_BIG)))
    kb = each(lambda i: kh[i] * beta[i])
    x = each(lambda i: -jnp.where(strict, _mm3(kb[i], kh[i], nt=True) * decay[i], 0.0))
    tinv = each(lambda i: eye + x[i])
    p = x
    for _ in range(N_LEVELS - 1):
        p = each(lambda i: _mm3(p[i], p[i]))
        tinv = each(lambda i: tinv[i] + _mm3(tinv[i], p[i]))
    egc = each(lambda i: jnp.exp(gc[i]))
    u = each(lambda i: _mm3(tinv[i], vh[i] * beta[i]))
    w = each(lambda i: _mm3(tinv[i], kb[i] * egc[i]))
    s = each(lambda i: s_ref[pairs[i][0], pairs[i][1]])
    v_new = each(lambda i: u[i] - _mm3(w[i], s[i]))
    bf = lambda a: a.astype(BF16)
    attn = each(lambda i: jnp.where(causal, _dot_nt(bf(qh[i]), bf(kh[i])) * decay[i], 0.0))
    o = each(lambda i: _dot(bf(qh[i] * egc[i]), bf(s[i])) + _dot(bf(attn[i]), bf(v_new[i])))
    s_new = each(lambda i: s[i] * jnp.exp(gl[i]) + _mm3((kh[i] * jnp.exp(gl[i] - gc[i])).T, v_new[i]))
    for i, (bb, h) in enumerate(pairs):
        sl = slice(h * HEAD_DV, (h + 1) * HEAD_DV)
        s_ref[bb, h] = s_new[i]
        o_ref[bb, :, sl] = _gated_out(o[i], z_ref[bb, :, sl], nw_ref[...]).astype(o_ref.dtype)


def gdn_mixer(proj, conv_w, a_log, dt_bias, nw, cc):
    _, _, _, tril, trilt = cc
    cw = jnp.zeros((3, 8, GROUP_W), F32).at[:, 0:4].set(conv_w.reshape(4, 3, GROUP_W).transpose(1, 0, 2))
    alog = jnp.zeros((1, 128), F32).at[0, 0:N_HEADS].set(a_log)
    dtb = jnp.zeros((1, 128), F32).at[0, 0:N_HEADS].set(dt_bias)
    return _mixer_call(
        _gdn_kernel, proj,
        [_pspec(P_AQ), _pspec(P_AK), _pspec(P_AV), _pspec(P_AZ), _pspec(P_SM * 4, 128)],
        [cw, alog, dtb, nw.reshape(1, HEAD_DV), tril, trilt],
        [(3, 8, GROUP_W), (N_HEADS, HEAD_DV, HEAD_DV)], "gdn_mixer", per_row=False)


def _shift_rows(x, d, fill):
    rows = lax.broadcasted_iota(jnp.int32, x.shape, 0)
    return jnp.where(rows >= d, pltpu.roll(x, d, 0), fill)


def _lru_kernel(x_ref, g_ref, cw_ref, cb_ref, wa_ref, ba_ref, wx_ref, bx_ref, lam_ref, o_ref, tail_ref, h_ref):
    @pl.when(pl.program_id(1) == 0)
    def _():
        tail_ref[...] = jnp.zeros_like(tail_ref)
        h_ref[...] = jnp.zeros_like(h_ref)

    xb = _causal_conv(x_ref, tail_ref, 0, cw_ref[...]) + cb_ref[...]
    xb_bf = xb.astype(BF16)
    bw = GROUP_W // LRU_BLOCKS
    ra = jnp.concatenate([_dot(xb_bf[:, g * bw:(g + 1) * bw], wa_ref[g]) for g in range(LRU_BLOCKS)], axis=1)
    rx = jnp.concatenate([_dot(xb_bf[:, g * bw:(g + 1) * bw], wx_ref[g]) for g in range(LRU_BLOCKS)], axis=1)
    rg = _sigmoid(ra + ba_ref[...])
    ig = _sigmoid(rx + bx_ref[...])
    log_a = -LRU_C * rg * _softplus(-lam_ref[...])
    a = jnp.exp(log_a)
    b = jnp.sqrt(1.0 - jnp.exp(2.0 * log_a)) * ig * xb
    d = 1
    while d < CHUNK:
        b = a * _shift_rows(b, d, 0.0) + b
        a = a * _shift_rows(a, d, 1.0)
        d *= 2
    h = b + a * h_ref[...]
    h_ref[...] = h[CHUNK - 1:CHUNK]
    o_ref[...] = (h * _gelu_tanh(g_ref[...])).astype(o_ref.dtype)


def lru_mixer(proj, conv_w, conv_b, w_a, b_a, w_x, b_x, lam):
    cw = jnp.zeros((8, GROUP_W), F32).at[0:4].set(conv_w)
    row = lambda v: v.reshape(1, GROUP_W)
    return _mixer_call(
        _lru_kernel, proj, [_pspec(P_CX), _pspec(P_CG)],
        [cw, row(conv_b), w_a.astype(BF16), row(b_a), w_x.astype(BF16), row(b_x), row(lam)],
        [(1, 8, GROUP_W), (1, GROUP_W)], "rglru_mixer")


ROUTE_TB = 256


def _topk_rows(s, k):
    n = s.shape[0]
    rows = lax.broadcasted_iota(jnp.int32, s.shape, 0)
    vals, ids = [], []
    for _ in range(k):
        m = jnp.max(s, axis=0, keepdims=True)
        idx = jnp.min(jnp.where(s == m, rows, n), axis=0, keepdims=True)
        vals.append(m)
        ids.append(idx)
        s = jnp.where(rows == idx, NEG_BIG, s)
    return jnp.concatenate(vals, axis=0), jnp.concatenate(ids, axis=0)


def _topk_rows_sorted(s, k):
    n, tb = s.shape
    ng = n // SUBLANES
    assert ng & (ng - 1) == 0
    sub = lax.broadcasted_iota(jnp.int32, (SUBLANES, tb), 0)
    v = [s[g * SUBLANES:(g + 1) * SUBLANES] for g in range(ng)]
    ix = [sub + g * SUBLANES for g in range(ng)]
    size = 2
    while size <= ng:
        stride = size // 2
        while stride >= 1:
            for i in range(ng):
                j = i ^ stride
                if j > i:
                    a, b = (i, j) if (i & size) == 0 else (j, i)
                    first = (v[i] > v[j]) | ((v[i] == v[j]) & (ix[i] < ix[j]))
                    hi_v, lo_v = jnp.where(first, v[i], v[j]), jnp.where(first, v[j], v[i])
                    hi_i, lo_i = jnp.where(first, ix[i], ix[j]), jnp.where(first, ix[j], ix[i])
                    v[a], v[b], ix[a], ix[b] = hi_v, lo_v, hi_i, lo_i
            stride //= 2
        size *= 2
    vals, ids = [], []
    for _ in range(k):
        m = jnp.max(v[0], axis=0, keepdims=True)
        idx = jnp.min(jnp.where(v[0] == m, ix[0], n), axis=0, keepdims=True)
        vals.append(m)
        ids.append(idx)
        hit = ix[0] == idx
        for g in range(ng - 1):
            v[g] = jnp.where(hit, v[g + 1], v[g])
            ix[g] = jnp.where(hit, ix[g + 1], ix[g])
        v[ng - 1] = jnp.where(hit, NEG_BIG, v[ng - 1])
        ix[ng - 1] = jnp.where(hit, n, ix[ng - 1])
    return jnp.concatenate(vals, axis=0), jnp.concatenate(ids, axis=0)


def _route_kernel(x_ref, g_ref, wq_ref, sk_ref, ex_ref, gt_ref):
    h2 = _rms(x_ref[...], g_ref[...]).astype(BF16)
    q = _dot(h2, wq_ref[...])
    for h in range(PEER_HEADS):
        tops = []
        for p in range(2):
            col = (h * 2 + p) * PEER_NKEYS
            sc = _dot_nt(sk_ref[h * 2 + p], q[:, col:col + PEER_NKEYS].astype(BF16))
            tops.append(_topk_rows_sorted(sc, PEER_TOPK))
        (s1, i1), (s2, i2) = tops
        sub = lax.broadcasted_iota(jnp.int32, (8, s1.shape[1]), 0)
        half = PEER_TOPK // 2
        cs = [s1[0:1] + s2[0:half], s1[0:1] + s2[half:PEER_TOPK]]
        ci = [i1[0:1] * PEER_NKEYS + i2[0:half], i1[0:1] * PEER_NKEYS + i2[half:PEER_TOPK]]
        for a in range(1, half):
            n_valid = PEER_TOPK // (a + 1)
            cs.append(jnp.where(sub < n_valid, s1[a:a + 1] + s2[0:half], NEG_BIG))
            ci.append(i1[a:a + 1] * PEER_NKEYS + i2[0:half])
        cs.append(s1[half:PEER_TOPK] + s2[0:1])
        ci.append(i1[half:PEER_TOPK] * PEER_NKEYS + i2[0:1])
        cand_s = jnp.concatenate(cs, axis=0)
        cand_i = jnp.concatenate(ci, axis=0)
        best_s, best_pos = _topk_rows(cand_s, PEER_TOPK)
        pos = lax.broadcasted_iota(jnp.int32, cand_s.shape, 0)
        experts = jnp.concatenate(
            [jnp.sum(jnp.where(pos == best_pos[r:r + 1], cand_i, 0), axis=0, keepdims=True)
             for r in range(PEER_TOPK)], axis=0)
        ew = jnp.exp(best_s - best_s[0:1])
        gates = ew / jnp.sum(ew, axis=0, keepdims=True)
        ex_ref[h * PEER_TOPK:(h + 1) * PEER_TOPK, :] = experts
        gt_ref[h * PEER_TOPK:(h + 1) * PEER_TOPK, :] = gates


def peer_route(x, gain, wq_bf, sk_bf):
    m, d = x.shape
    nq = wq_bf.shape[1]
    assert m % ROUTE_TB == 0, m
    return pl.pallas_call(
        _route_kernel,
        grid=(m // ROUTE_TB,),
        in_specs=[pl.BlockSpec((ROUTE_TB, d), lambda i: (i, 0)),
                  pl.BlockSpec((1, d), lambda i: (0, 0)),
                  pl.BlockSpec((d, nq), lambda i: (0, 0)),
                  pl.BlockSpec(sk_bf.shape, lambda i: (0, 0, 0))],
        out_specs=[pl.BlockSpec((PEER_SEL, ROUTE_TB), lambda i: (0, i)),
                   pl.BlockSpec((PEER_SEL, ROUTE_TB), lambda i: (0, i))],
        out_shape=[jax.ShapeDtypeStruct((PEER_SEL, m), jnp.int32),
                   jax.ShapeDtypeStruct((PEER_SEL, m), F32)],
        compiler_params=_params("parallel"),
        name="peer_route",
    )(x, gain.reshape(1, d), wq_bf, sk_bf)


EXP_TB = 64
GATE_W = 128
EXP_TG = 8
EXP_SLOTS = 4
EXP_AHEAD = 2
SUBLANES = 8
EXP_ROWS = EXP_TG * PEER_SEL
EXP_TILE_ROWS = EXP_ROWS // SUBLANES
TOK_TILE_ROWS = PEER_SEL // SUBLANES


def _expert_kernel(ids_ref, nxt_ref, x_ref, g_ref, gt_ref, uv_ref, o_ref, *scratch):
    bufs, (h_ref, sem) = scratch[:EXP_SLOTS], scratch[EXP_SLOTS:]
    d = x_ref.shape[1]
    n_word = d // 256
    n_sub = EXP_TB // EXP_TG
    step = pl.program_id(0)
    lane0 = (step % (GATE_W // EXP_TB)) * EXP_TB
    rows_per_iter = EXP_TILE_ROWS // (2 * EXP_TG)
    h_ref[...] = _rms(x_ref[...], g_ref[...])

    def issue_tile_row(src_ids, sub, slot, i):
        base = sub * EXP_ROWS + i * SUBLANES
        for k in range(SUBLANES):
            e = src_ids[base + k]
            pltpu.make_async_copy(uv_ref.at[e], bufs[slot].at[i, :, k, :], sem.at[slot]).start(priority=k % 2)

    def issue_all(src_ids, sub, slot):
        def body(i, carry):
            issue_tile_row(src_ids, sub, slot, i)
            return carry
        lax.fori_loop(0, EXP_TILE_ROWS, body, 0)

    def wait(slot):
        pltpu.make_async_copy(bufs[(slot + 1) % EXP_SLOTS], bufs[slot], sem.at[slot]).wait()

    lane = lax.broadcasted_iota(jnp.int32, (PEER_SEL, GATE_W), 1)

    def compute(sub, slot, ahead):
        def request(first_row):
            if ahead is not None:
                for ii in range(rows_per_iter):
                    issue_tile_row(ahead[0], ahead[1], ahead[2], first_row + ii)

        def u_side(t, act):
            n = sub * EXP_TG + t
            request(t * rows_per_iter)
            xn = h_ref[pl.ds(n, 1), :]
            parts = []
            for g in range(TOK_TILE_ROWS):
                acc = None
                for c in range(n_word):
                    lo, hi = _unpack_pair(bufs[slot][t * TOK_TILE_ROWS + g, c])
                    term = (lo * xn[:, c * 128:(c + 1) * 128]
                            + hi * xn[:, d // 2 + c * 128:d // 2 + (c + 1) * 128])
                    acc = term if acc is None else acc + term
                parts.append(acc)
            s_col = jnp.sum(jnp.concatenate(parts, axis=0), axis=-1, keepdims=True)
            return jnp.where(lane == lane0 + n, s_col, act)

        act = lax.fori_loop(0, EXP_TG, u_side, jnp.zeros((PEER_SEL, GATE_W), F32))
        coeff = gt_ref[...] * _gelu_tanh(act)

        def v_side(t, carry):
            n = sub * EXP_TG + t
            request((EXP_TG + t) * rows_per_iter)
            c_col = jnp.sum(jnp.where(lane == lane0 + n, coeff, 0.0), axis=-1, keepdims=True)
            c_b = jnp.broadcast_to(c_col, (PEER_SEL, 128))
            outs_lo, outs_hi = [], []
            for c in range(n_word):
                acc_lo = acc_hi = None
                for g in range(TOK_TILE_ROWS):
                    lo, hi = _unpack_pair(bufs[slot][t * TOK_TILE_ROWS + g, n_word + c])
                    cg = c_b[g * SUBLANES:(g + 1) * SUBLANES]
                    acc_lo = lo * cg if acc_lo is None else acc_lo + lo * cg
                    acc_hi = hi * cg if acc_hi is None else acc_hi + hi * cg
                outs_lo.append(jnp.sum(acc_lo, axis=0, keepdims=True))
                outs_hi.append(jnp.sum(acc_hi, axis=0, keepdims=True))
            o_ref[pl.ds(n, 1), :] = x_ref[pl.ds(n, 1), :] + jnp.concatenate(outs_lo + outs_hi, axis=1)
            return carry

        lax.fori_loop(0, EXP_TG, v_side, 0)

    @pl.when(step == 0)
    def _():
        for s in range(EXP_AHEAD):
            issue_all(ids_ref, s, s)

    def group(q, carry):
        for r in range(EXP_SLOTS):
            sub = q * EXP_SLOTS + r
            wait(r)
            nxt_slot = (r + EXP_AHEAD) % EXP_SLOTS
            if r + EXP_AHEAD < EXP_SLOTS:
                compute(sub, r, (ids_ref, sub + EXP_AHEAD, nxt_slot))
            else:
                in_step = sub + EXP_AHEAD < n_sub
                pl.when(in_step)(functools.partial(compute, sub, r, (ids_ref, sub + EXP_AHEAD, nxt_slot)))
                pl.when(jnp.logical_not(in_step))(
                    functools.partial(compute, sub, r, (nxt_ref, r + EXP_AHEAD - EXP_SLOTS, nxt_slot)))
        return carry

    lax.fori_loop(0, n_sub // EXP_SLOTS, group, 0)

    @pl.when(step + 1 == pl.num_programs(0))
    def _():
        for s in range(EXP_AHEAD):
            wait(s)


def _pack_bf16_pairs(t):
    e, d = t.shape
    bits = lax.bitcast_convert_type(t.astype(BF16), jnp.uint16).astype(jnp.uint32)
    words = bits[:, :d // 2] | (bits[:, d // 2:] << 16)
    return words.reshape(e, d // 256, 128)


def _unpack_pair(w):
    lo = lax.bitcast_convert_type(w << 16, F32)
    hi = lax.bitcast_convert_type(w & jnp.uint32(0xFFFF0000), F32)
    return lo, hi


def peer_experts(x, gain, ids_flat, gates_t, u, v):
    m, d = x.shape
    assert m % GATE_W == 0 and d % 256 == 0, (m, d)
    n_blk = m // EXP_TB
    uv_tiles = jnp.concatenate([_pack_bf16_pairs(u), _pack_bf16_pairs(v)], axis=1)
    ids_spec = lambda index_map: pl.BlockSpec((EXP_TB * PEER_SEL,), index_map, memory_space=pltpu.SMEM)
    return pl.pallas_call(
        _expert_kernel,
        grid=(n_blk,),
        in_specs=[ids_spec(lambda i: (i,)),
                  ids_spec(lambda i: (jnp.minimum(i + 1, n_blk - 1),)),
                  pl.BlockSpec((EXP_TB, d), lambda i: (i, 0)),
                  pl.BlockSpec((1, d), lambda i: (0, 0)),
                  pl.BlockSpec((PEER_SEL, GATE_W), lambda i: (0, i // (GATE_W // EXP_TB))),
                  pl.BlockSpec(memory_space=pl.ANY)],
        out_specs=pl.BlockSpec((EXP_TB, d), lambda i: (i, 0)),
        out_shape=jax.ShapeDtypeStruct((m, d), F32),
        scratch_shapes=[pltpu.VMEM((EXP_TILE_ROWS, 2 * d // 256, SUBLANES, 128), jnp.uint32)] * EXP_SLOTS + [
                        pltpu.VMEM((EXP_TB, d), F32),
                        pltpu.SemaphoreType.DMA((EXP_SLOTS,))],
        compiler_params=_params("arbitrary"),
        name="peer_experts",
    )(ids_flat, ids_flat, x, gain.reshape(1, d), gates_t, uv_tiles)


def _relayout_w_in(w):
    d = w.shape[0]
    z = lambda n: jnp.zeros((d, n), w.dtype)
    return jnp.concatenate([
        w[:, 0:2048],
        w[:, 2056:4104],
        w[:, 4104:5128],
        w[:, 5640:6664],
        w[:, 5128:5640],
        w[:, 2048:2056], z(120),
        w[:, 6664:6680], z(112),
        z(256)], axis=1)


def kernel(x, norm_mix, norm_ffn, norm_final, w_in, w_out, gdn_conv_w, gdn_a_log, gdn_dt_bias, gdn_norm,
           hgrn_lb_logits, hgrn_norm, lru_conv_w, lru_conv_b, lru_w_a, lru_b_a, lru_w_x, lru_b_x, lru_lambda,
           gla_w_gate, gla_b_gate, gla_norm, peer_w_q, peer_sub_keys, peer_u, peer_v):
    bsz, t, d = x.shape
    m = bsz * t
    depth = w_in.shape[0]
    cc = _chunk_consts()
    lb_all = jnp.cumsum(jax.nn.softmax(hgrn_lb_logits.astype(F32), axis=0), axis=0)
    lb_all = lb_all - lb_all[0]
    tm = 512 if m % 512 == 0 else 256
    xf = x.reshape(m, d)
    for l in range(depth):
        w_in_l = _relayout_w_in(w_in[l]).astype(BF16)
        tm_in = 2 * tm if m % (2 * tm) == 0 else tm
        proj = norm_matmul(xf, norm_mix[l], w_in_l, tm_in, P_WIDTH // 4).reshape(bsz, t, P_WIDTH)
        y_a = gdn_mixer(proj, gdn_conv_w[l], gdn_a_log[l], gdn_dt_bias[l], gdn_norm[l], cc)
        y_b = hgrn_mixer(proj, lb_all[l], hgrn_norm[l], cc)
        y_c = lru_mixer(proj, lru_conv_w[l], lru_conv_b[l], lru_w_a[l], lru_b_a[l], lru_w_x[l], lru_b_x[l],
                        lru_lambda[l])
        y_d = gla_mixer(proj, gla_w_gate[l], gla_b_gate[l], gla_norm[l], cc)
        ys = [y.reshape(m, GROUP_W) for y in (y_a, y_b, y_c, y_d)]
        xf = out_proj(ys, w_out[l].reshape(4, GROUP_W, d).astype(BF16), xf, tm)
        sk = peer_sub_keys[l].reshape(PEER_HEADS * 2, PEER_NKEYS, -1).astype(BF16)
        experts_t, gates_t = peer_route(xf, norm_ffn[l], peer_w_q[l].astype(BF16), sk)
        ids_flat = experts_t.T.reshape(m * PEER_SEL)
        xf = peer_experts(xf, norm_ffn[l], ids_flat, gates_t, peer_u[l], peer_v[l])
    return final_norm(xf, norm_final, tm).reshape(bsz, t, d)
```
